```python
import jax, jax.numpy as jnp
from jax import lax
import numpy as np

D_MODEL = 2048
BATCH = 4
SEQ = 4096
DEPTH = 1
DEC_BATCH = 1
DEC_SEQ = 16384
PAST_LEN = 128

GRID_W = 64
D_CONV = D_MODEL // 2
CONV_K = 3
N_HEADS = 16
HEAD_DIM = 64
D_ATTN = N_HEADS * HEAD_DIM
WIN_R = 8
WIN_C = 16
RMS_EPS = 1e-6
IN_SIZES = (D_CONV, D_CONV, D_CONV, D_CONV,
            D_ATTN, D_ATTN, D_ATTN, D_ATTN,
            D_MODEL, D_MODEL)
D_IN = sum(IN_SIZES)
IN_SPLITS = tuple(int(s) for s in np.cumsum(IN_SIZES)[:-1])

kernel_name = "hybrid_shortconv_neighbourhood_attn_encoder"


def _rms(x, g):
    xf = x.astype(jnp.float32)
    xf = xf * lax.rsqrt(jnp.mean(xf * xf, axis=-1, keepdims=True) + RMS_EPS)
    return (xf * g.astype(jnp.float32)).astype(x.dtype)


def _neigh_attn(q, k, v, rpb):
    b, L, h, dh = q.shape
    rows = L // GRID_W
    wr = min(WIN_R, rows)
    q = q.reshape(b, rows, GRID_W, h, dh)
    k = k.reshape(b, rows, GRID_W, h, dh)
    v = v.reshape(b, rows, GRID_W, h, dh)
    cols = jnp.arange(GRID_W)
    col_start = jnp.clip(cols - WIN_C // 2, 0, GRID_W - WIN_C)
    col_idx = col_start[:, None] + jnp.arange(WIN_C)[None, :]
    dc_idx = col_idx - cols[:, None] + (WIN_C - 1)
    scale = HEAD_DIM ** -0.5

    def row_step(r):
        rs = jnp.clip(r - wr // 2, 0, rows - wr)
        k_rows = lax.dynamic_slice_in_dim(k, rs, wr, axis=1)
        v_rows = lax.dynamic_slice_in_dim(v, rs, wr, axis=1)
        k_win = k_rows[:, :, col_idx]
        v_win = v_rows[:, :, col_idx]
        q_r = lax.dynamic_index_in_dim(q, r, axis=1, keepdims=False)
        s = jnp.einsum('bchd,bicjhd->bhcij', q_r, k_win).astype(jnp.float32) * scale
        dr_idx = rs + jnp.arange(wr) - r + (WIN_R - 1)
        bias = rpb[:, dr_idx][:, :, dc_idx]
        s = s + jnp.transpose(bias, (0, 2, 1, 3)).astype(jnp.float32)[None]
        p = jax.nn.softmax(s.reshape(b, h, GRID_W, wr * WIN_C), axis=-1)
        p = p.reshape(b, h, GRID_W, wr, WIN_C).astype(v.dtype)
        return jnp.einsum('bhcij,bicjhd->bchd', p, v_win)

    out = lax.map(row_step, jnp.arange(rows))
    return jnp.transpose(out, (1, 0, 2, 3, 4)).reshape(b, L, h * dh)


def _layer(x, c, norm_g, w_ada, b_ada, w_in, conv_w, conv_b, q_norm_g, k_norm_g,
           rpb, w_pa, w_pb, w_o):
    bsz, L, _ = x.shape
    mod = jax.nn.silu(c) @ w_ada + b_ada
    shift, scl, gate = jnp.split(mod, 3, axis=-1)
    h = _rms(x, norm_g) * (1 + scl[:, None, :]) + shift[:, None, :]
    z = h @ w_in
    a_b, a_c, a_x, a_z, q, k, v, b_z, g_a, g_b = jnp.split(z, IN_SPLITS, axis=-1)
    u = a_c * a_x
    pad = CONV_K // 2
    up = jnp.pad(u, ((0, 0), (pad, CONV_K - 1 - pad), (0, 0)))
    conv = sum(up[:, i:i + L] * conv_w[i] for i in range(CONV_K)) + conv_b
    y_a = a_b * conv * jax.nn.silu(a_z)
    q = _rms(q.reshape(bsz, L, N_HEADS, HEAD_DIM), q_norm_g)
    k = _rms(k.reshape(bsz, L, N_HEADS, HEAD_DIM), k_norm_g)
    v = v.reshape(bsz, L, N_HEADS, HEAD_DIM)
    y_b = _neigh_attn(q, k, v, rpb) * jax.nn.silu(b_z)
    m = jax.nn.sigmoid(g_a) * (y_a @ w_pa) + jax.nn.sigmoid(g_b) * (y_b @ w_pb)
    return x + gate[:, None, :] * (m @ w_o)


def setup_inputs(seed: int = 0) -> dict:
    key = jax.random.key(seed)
    ks = jax.random.split(key, 16)
    f32 = jnp.float32
    n = lambda k, shape, s: (jax.random.normal(k, shape, f32) * s).astype(f32)
    return {
        "x_prompt": n(ks[0], (BATCH, SEQ, D_MODEL), 1.0),
        "x_sample": n(ks[1], (DEC_BATCH, DEC_SEQ, D_MODEL), 1.0),
        "c_prompt": n(ks[2], (BATCH, D_MODEL), 1.0),
        "c_sample": n(ks[3], (DEC_BATCH, D_MODEL), 1.0),
        "norm_g": 1.0 + n(ks[4], (DEPTH, D_MODEL), 0.01),
        "w_ada": n(ks[5], (DEPTH, D_MODEL, 3 * D_MODEL), 0.5 * D_MODEL ** -0.5),
        "b_ada": n(ks[6], (DEPTH, 3 * D_MODEL), 0.01),
        "w_in": n(ks[7], (DEPTH, D_MODEL, D_IN), D_MODEL ** -0.5),
        "conv_w": n(ks[8], (DEPTH, CONV_K, D_CONV), CONV_K ** -0.5),
        "conv_b": n(ks[9], (DEPTH, D_CONV), 0.01),
        "q_norm_g": 1.0 + n(ks[10], (DEPTH, HEAD_DIM), 0.01),
        "k_norm_g": 1.0 + n(ks[11], (DEPTH, HEAD_DIM), 0.01),
        "rpb": n(ks[12], (DEPTH, N_HEADS, 2 * WIN_R - 1, 2 * WIN_C - 1), 0.1),
        "w_pa": n(ks[13], (DEPTH, D_CONV, D_MODEL), D_CONV ** -0.5),
        "w_pb": n(ks[14], (DEPTH, D_ATTN, D_MODEL), D_ATTN ** -0.5),
        "w_o": n(ks[15], (DEPTH, D_MODEL, D_MODEL), D_MODEL ** -0.5),
    }


def reference(x_prompt, x_sample, c_prompt, c_sample, norm_g, w_ada, b_ada, w_in,
              conv_w, conv_b, q_norm_g, k_norm_g, rpb, w_pa, w_pb, w_o):
    y_prompt = x_prompt
    y_sample = x_sample
    for l in range(DEPTH):
        params = (norm_g[l], w_ada[l], b_ada[l], w_in[l], conv_w[l], conv_b[l],
                  q_norm_g[l], k_norm_g[l], rpb[l], w_pa[l], w_pb[l], w_o[l])
        y_prompt = _layer(y_prompt, c_prompt, *params)
        y_sample = _layer(y_sample, c_sample, *params)
    return (y_prompt, y_sample)
```

```python
import functools

import jax
import jax.numpy as jnp
from jax import lax
from jax.experimental import pallas as pl
from jax.experimental.pallas import tpu as pltpu

F32 = jnp.float32
BF16 = jnp.bfloat16

GRID_W = 64
N_HEADS = 16
HEAD_DIM = 64
WIN_R = 8
WIN_C = 16
CONV_K = 3
RMS_EPS = 1e-6

LANES = 128
MXU_DIM = 256
HEADS_PER_GROUP = MXU_DIM // HEAD_DIM
N_GROUPS = N_HEADS // HEADS_PER_GROUP
VMEM_LIMIT_BYTES = 56 * 1024 * 1024

MASK_VALUE = -1e30

COL = 1024
ZB_AB, ZB_U, ZB_SZ, ZB_Q, ZB_K, ZB_V, ZB_SBZ, ZB_SGA, ZB_SGB = 0, 1, 2, 3, 4, 5, 6, 7, 9
N_ZBLOCKS = 11
PJ_AB, PJ_AC, PJ_AX, PJ_AZ, PJ_Q, PJ_K, PJ_V, PJ_BZ, PJ_G0 = 0, 1, 2, 3, 4, 5, 6, 7, 8
N_PJ = 12


def _silu(x):
    return x * jax.nn.sigmoid(x)


def _mod_kernel(c_ref, w_ref, b_ref, o_ref):
    s = _silu(c_ref[...])
    o_ref[...] = jnp.dot(s.astype(BF16), w_ref[...].astype(BF16),
                         preferred_element_type=F32) + b_ref[...]


def _mod_call(c_all, w_ada, b_ada):
    nb, d = c_all.shape
    n = w_ada.shape[1]
    tn = 1024
    return pl.pallas_call(
        _mod_kernel,
        grid=(n // tn,),
        in_specs=[
            pl.BlockSpec((nb, d), lambda j: (0, 0)),
            pl.BlockSpec((d, tn), lambda j: (0, j)),
            pl.BlockSpec((1, tn), lambda j: (0, j)),
        ],
        out_specs=pl.BlockSpec((nb, tn), lambda j: (0, j)),
        out_shape=jax.ShapeDtypeStruct((nb, n), F32),
        compiler_params=pltpu.CompilerParams(
            dimension_semantics=("arbitrary",), vmem_limit_bytes=VMEM_LIMIT_BYTES),
        name="mod",
    )(c_all, w_ada, b_ada.reshape(1, n))


def _bias_kernel(rpb_ref, o_ref):
    h = pl.program_id(0)
    shape = (GRID_W, 2 * GRID_W)
    c = lax.broadcasted_iota(jnp.int32, shape, 0)
    l = lax.broadcasted_iota(jnp.int32, shape, 1)
    upper = l >= GRID_W
    kc = jnp.where(upper, l - GRID_W, l)
    cs = jnp.clip(c - WIN_C // 2, 0, GRID_W - WIN_C)
    valid = (kc >= cs) & (kc < cs + WIN_C)
    e_idx = kc - c + (WIN_C - 1)
    for d in range(2 * WIN_R - 2):
        acc = jnp.full(shape, MASK_VALUE, F32)
        for e in range(2 * WIN_C - 1):
            val = jnp.where(upper, rpb_ref[h, d + 1, e], rpb_ref[h, d, e])
            acc = jnp.where(valid & (e_idx == e), val, acc)
        o_ref[0, d] = acc


def _bias_call(rpb):
    nh = rpb.shape[0]
    nd = 2 * WIN_R - 2
    return pl.pallas_call(
        _bias_kernel,
        grid=(nh,),
        in_specs=[pl.BlockSpec(memory_space=pltpu.SMEM)],
        out_specs=pl.BlockSpec((1, nd, GRID_W, 2 * GRID_W), lambda h: (h, 0, 0, 0)),
        out_shape=jax.ShapeDtypeStruct((nh, nd, GRID_W, 2 * GRID_W), F32),
        compiler_params=pltpu.CompilerParams(dimension_semantics=("arbitrary",)),
        name="bias",
    )(rpb)


def _head_rms(o, gain, scale):
    r = lax.broadcasted_iota(jnp.int32, (MXU_DIM, MXU_DIM), 0) // HEAD_DIM
    c = lax.broadcasted_iota(jnp.int32, (MXU_DIM, MXU_DIM), 1) // HEAD_DIM
    bd = jnp.where(r == c, 1.0, 0.0).astype(BF16)
    outs = []
    for k in range(o.shape[1] // MXU_DIM):
        oc = o[:, k * MXU_DIM:(k + 1) * MXU_DIM]
        ss = jnp.dot((oc * oc).astype(BF16), bd, preferred_element_type=F32)
        outs.append(oc * lax.rsqrt(ss * (1.0 / HEAD_DIM) + RMS_EPS))
    on = jnp.concatenate(outs, axis=1) * gain
    return on * scale if scale != 1.0 else on


def _proj_kernel(x_ref, g_ref, mod_ref, w_ref, qg_ref, kg_ref, o_ref, h_scr, ac_scr,
                 *, tm, chunk):
    j = pl.program_id(1)

    @pl.when(j == 0)
    def _():
        shift = mod_ref[0, 0:1, :]
        scl1 = 1.0 + mod_ref[0, 1:2, :]
        g = g_ref[...]

        def body(ci, carry):
            r0 = pl.multiple_of(ci * chunk, chunk)
            x = x_ref[pl.ds(r0, chunk), :]
            ms = jnp.mean(x * x, axis=-1, keepdims=True)
            xn = x * lax.rsqrt(ms + RMS_EPS)
            h_scr[pl.ds(r0, chunk), :] = ((xn * g) * scl1 + shift).astype(BF16)
            return carry

        lax.fori_loop(0, tm // chunk, body, 0)

    o = jnp.dot(h_scr[...], w_ref[...], preferred_element_type=F32)

    @pl.when((j == PJ_AB) | (j == PJ_V))
    def _():
        o_ref[...] = o.astype(BF16)

    @pl.when(j == PJ_AC)
    def _():
        ac_scr[...] = o

    @pl.when(j == PJ_AX)
    def _():
        o_ref[...] = (ac_scr[...] * o).astype(BF16)

    @pl.when((j == PJ_AZ) | (j == PJ_BZ))
    def _():
        o_ref[...] = _silu(o).astype(BF16)

    @pl.when(j == PJ_Q)
    def _():
        o_ref[...] = _head_rms(o, qg_ref[...], HEAD_DIM ** -0.5).astype(BF16)

    @pl.when(j == PJ_K)
    def _():
        o_ref[...] = _head_rms(o, kg_ref[...], 1.0).astype(BF16)

    @pl.when(j >= PJ_G0)
    def _():
        o_ref[...] = jax.nn.sigmoid(o).astype(BF16)


def _proj_call(x2d, norm_g, mod3, w_in_bf, qg_t, kg_t, *, batch_of_tile, tm):
    t, d = x2d.shape
    n = w_in_bf.shape[1]
    assert n == N_PJ * COL and t % tm == 0
    kern = functools.partial(_proj_kernel, tm=tm, chunk=128)
    return pl.pallas_call(
        kern,
        grid=(t // tm, N_PJ),
        in_specs=[
            pl.BlockSpec((tm, d), lambda i, j: (i, 0)),
            pl.BlockSpec((1, d), lambda i, j: (0, 0)),
            pl.BlockSpec((1, 3, d), lambda i, j: (batch_of_tile(i), 0, 0)),
            pl.BlockSpec((d, COL), lambda i, j: (0, j)),
            pl.BlockSpec((1, COL), lambda i, j: (0, 0)),
            pl.BlockSpec((1, COL), lambda i, j: (0, 0)),
        ],
        out_specs=pl.BlockSpec((tm, COL), lambda i, j: (i, jnp.where(j >= PJ_AX, j - 1, j))),
        out_shape=jax.ShapeDtypeStruct((t, N_ZBLOCKS * COL), BF16),
        scratch_shapes=[pltpu.VMEM((tm, d), BF16), pltpu.VMEM((tm, COL), F32)],
        compiler_params=pltpu.CompilerParams(
            dimension_semantics=("arbitrary", "arbitrary"),
            vmem_limit_bytes=VMEM_LIMIT_BYTES),
        name="proj",
    )(x2d, norm_g.reshape(1, d), mod3, w_in_bf, qg_t, kg_t)


def _attn_kernel(q_ref, kp_ref, kc_ref, kn_ref, vp_ref, vc_ref, vn_ref, bz_ref, t2_ref,
                 o_ref, kbuf, vbuf, *, rb, rows):
    i = pl.program_id(1)
    tb = rb * GRID_W
    kbuf[0:tb] = kp_ref[...]
    kbuf[tb:2 * tb] = kc_ref[...]
    kbuf[2 * tb:3 * tb] = kn_ref[...]
    vbuf[0:tb] = vp_ref[...]
    vbuf[tb:2 * tb] = vc_ref[...]
    vbuf[2 * tb:3 * tb] = vn_ref[...]

    lane_head = lax.broadcasted_iota(jnp.int32, (GRID_W, MXU_DIM), 1) // HEAD_DIM
    nk = WIN_R * GRID_W

    def row_body(jr, carry):
        r = i * rb + jr
        rs = jnp.clip(r - WIN_R // 2, 0, rows - WIN_R)
        start = pl.multiple_of((rs - (i - 1) * rb) * GRID_W, GRID_W)
        dbase = rs - r + (WIN_R - 1)
        q_off = pl.multiple_of(jr * GRID_W, GRID_W)
        for g in range(N_GROUPS):
            cs = slice(g * MXU_DIM, (g + 1) * MXU_DIM)
            qg = q_ref[pl.ds(q_off, GRID_W), cs]
            zero = jnp.zeros_like(qg)
            q4 = jnp.concatenate(
                [jnp.where(lane_head == h, qg, zero) for h in range(HEADS_PER_GROUP)], axis=0)
            kw = kbuf[pl.ds(start, nk), cs]
            s = lax.dot_general(q4, kw, (((1,), (1,)), ((), ())),
                                preferred_element_type=F32)
            cols = []
            for cb in range(nk // LANES):
                bias = jnp.concatenate(
                    [t2_ref[g * HEADS_PER_GROUP + h, dbase + 2 * cb]
                     for h in range(HEADS_PER_GROUP)], axis=0)
                cols.append(s[:, cb * LANES:(cb + 1) * LANES] + bias)
            s = jnp.concatenate(cols, axis=1)
            m = jnp.max(s, axis=-1, keepdims=True)
            p = jnp.exp(s - m)
            den = jnp.sum(p, axis=-1, keepdims=True)
            vw = vbuf[pl.ds(start, nk), cs]
            pv = jnp.dot(p.astype(BF16), vw, preferred_element_type=F32)
            pv = pv * (1.0 / den)
            o = jnp.zeros((GRID_W, MXU_DIM), F32)
            for h in range(HEADS_PER_GROUP):
                o = o + jnp.where(lane_head == h, pv[h * GRID_W:(h + 1) * GRID_W], 0.0)
            sbz = bz_ref[pl.ds(q_off, GRID_W), cs].astype(F32)
            o_ref[pl.ds(q_off, GRID_W), cs] = (o * sbz).astype(BF16)
        return carry

    lax.fori_loop(0, rb, row_body, 0)


def _attn_call(z, t2, *, n_batch, seq, rb):
    rows = seq // GRID_W
    nblk = rows // rb
    tb = rb * GRID_W
    assert rows % rb == 0 and rb >= WIN_R and nblk >= 2
    kern = functools.partial(_attn_kernel, rb=rb, rows=rows)

    def cur(col):
        return pl.BlockSpec((tb, COL), lambda b, i: (b * nblk + i, col))

    def prev(col):
        return pl.BlockSpec((tb, COL), lambda b, i: (b * nblk + jnp.maximum(i - 1, 0), col))

    def nxt(col):
        return pl.BlockSpec((tb, COL), lambda b, i: (b * nblk + jnp.minimum(i + 1, nblk - 1), col))

    return pl.pallas_call(
        kern,
        grid=(n_batch, nblk),
        in_specs=[cur(ZB_Q), prev(ZB_K), cur(ZB_K), nxt(ZB_K), prev(ZB_V), cur(ZB_V), nxt(ZB_V),
                  cur(ZB_SBZ),
                  pl.BlockSpec(t2.shape, lambda b, i: (0, 0, 0, 0))],
        out_specs=pl.BlockSpec((tb, COL), lambda b, i: (b * nblk + i, 0)),
        out_shape=jax.ShapeDtypeStruct((n_batch * seq, COL), BF16),
        scratch_shapes=[pltpu.VMEM((3 * tb, COL), BF16), pltpu.VMEM((3 * tb, COL), BF16)],
        compiler_params=pltpu.CompilerParams(
            dimension_semantics=("arbitrary", "arbitrary"),
            vmem_limit_bytes=VMEM_LIMIT_BYTES),
        name="attn",
    )(z, z, z, z, z, z, z, z, t2)


def _out_kernel(ab_ref, u_ref, up_ref, un_ref, sz_ref, yb_ref, sga0_ref, sga1_ref, sgb0_ref,
                sgb1_ref, x_ref, mod_ref, cw_ref, cb_ref, wpa_ref, wpb_ref, wo_ref, o_ref,
                *, tm, tiles_per_seq, halo):
    t = pl.program_id(0) % tiles_per_seq
    u = u_ref[...].astype(F32)
    prev_row = jnp.where(t == 0, 0.0, up_ref[...].astype(F32)[halo - 1:halo, :])
    next_row = jnp.where(t == tiles_per_seq - 1, 0.0, un_ref[...].astype(F32)[0:1, :])
    row = lax.broadcasted_iota(jnp.int32, u.shape, 0)
    u_m1 = jnp.where(row == 0, prev_row, pltpu.roll(u, 1, 0))
    u_p1 = jnp.where(row == tm - 1, next_row, pltpu.roll(u, tm - 1, 0))
    conv = u_m1 * cw_ref[0:1, :] + u * cw_ref[1:2, :] + u_p1 * cw_ref[2:3, :] + cb_ref[...]
    ya = ab_ref[...].astype(F32) * conv * sz_ref[...].astype(F32)
    pa = jnp.dot(ya.astype(BF16), wpa_ref[...], preferred_element_type=F32)
    pb = jnp.dot(yb_ref[...], wpb_ref[...], preferred_element_type=F32)
    m0 = (sga0_ref[...].astype(F32) * pa[:, :COL] + sgb0_ref[...].astype(F32) * pb[:, :COL])
    m1 = (sga1_ref[...].astype(F32) * pa[:, COL:] + sgb1_ref[...].astype(F32) * pb[:, COL:])
    m = jnp.concatenate([m0.astype(BF16), m1.astype(BF16)], axis=1)
    o = jnp.dot(m, wo_ref[...], preferred_element_type=F32)
    o_ref[...] = x_ref[...] + mod_ref[0, 2:3, :] * o


def _out_call(z, yb, x2d, mod3, conv_w, conv_b, wpa_bf, wpb_bf, wo_bf, *, batch_of_tile,
              seq, tm):
    t, d = x2d.shape
    dc = wpa_bf.shape[0]
    halo = 16
    tiles_per_seq = seq // tm
    ntile = t // tm
    hb = tm // halo
    nhalo = t // halo
    kern = functools.partial(_out_kernel, tm=tm, tiles_per_seq=tiles_per_seq, halo=halo)

    def zcol(col):
        return pl.BlockSpec((tm, COL), lambda i: (i, col))

    const = lambda shape: pl.BlockSpec(shape, lambda i: (0,) * len(shape),
                                       pipeline_mode=pl.Buffered(1))
    return pl.pallas_call(
        kern,
        grid=(ntile,),
        in_specs=[
            zcol(ZB_AB), zcol(ZB_U),
            pl.BlockSpec((halo, COL), lambda i: (jnp.maximum(i * hb - 1, 0), ZB_U)),
            pl.BlockSpec((halo, COL), lambda i: (jnp.minimum((i + 1) * hb, nhalo - 1), ZB_U)),
            zcol(ZB_SZ),
            pl.BlockSpec((tm, COL), lambda i: (i, 0)),
            zcol(ZB_SGA), zcol(ZB_SGA + 1), zcol(ZB_SGB), zcol(ZB_SGB + 1),
            pl.BlockSpec((tm, d), lambda i: (i, 0)),
            pl.BlockSpec((1, 3, d), lambda i: (batch_of_tile(i), 0, 0)),
            const((CONV_K, dc)), const((1, dc)),
            const((dc, d)), const((dc, d)), const((d, d)),
        ],
        out_specs=pl.BlockSpec((tm, d), lambda i: (i, 0)),
        out_shape=jax.ShapeDtypeStruct((t, d), F32),
        compiler_params=pltpu.CompilerParams(
            dimension_semantics=("arbitrary",), vmem_limit_bytes=VMEM_LIMIT_BYTES),
        name="out",
    )(z, z, z, z, z, yb, z, z, z, z, x2d, mod3, conv_w, conv_b.reshape(1, dc),
      wpa_bf, wpb_bf, wo_bf)


def _stream(x, mod3, batch0, t2, norm_g, w_in_bf, qg_t, kg_t, conv_w, conv_b, wpa_bf, wpb_bf,
            wo_bf):
    nb, seq, d = x.shape
    x2d = x.reshape(nb * seq, d)
    tm_proj, tm_out, rb = 512, 256, 8
    z = _proj_call(x2d, norm_g, mod3, w_in_bf, qg_t, kg_t, tm=tm_proj,
                   batch_of_tile=lambda i: batch0 + i // (seq // tm_proj))
    yb = _attn_call(z, t2, n_batch=nb, seq=seq, rb=rb)
    y = _out_call(z, yb, x2d, mod3, conv_w, conv_b, wpa_bf, wpb_bf, wo_bf, seq=seq, tm=tm_out,
                  batch_of_tile=lambda i: batch0 + i // (seq // tm_out))
    return y.reshape(nb, seq, d)


def kernel(x_prompt, x_sample, c_prompt, c_sample, norm_g, w_ada, b_ada, w_in, conv_w, conv_b,
           q_norm_g, k_norm_g, rpb, w_pa, w_pb, w_o):
    depth = w_in.shape[0]
    d = x_prompt.shape[-1]
    nbp = c_prompt.shape[0]
    nbs = c_sample.shape[0]
    c_all = jnp.concatenate([c_prompt, c_sample], axis=0)
    pad = (-c_all.shape[0]) % 8
    c_all = jnp.pad(c_all, ((0, pad), (0, 0)))
    y_prompt, y_sample = x_prompt, x_sample
    for l in range(depth):
        mod3 = _mod_call(c_all, w_ada[l], b_ada[l]).reshape(c_all.shape[0], 3, d)
        t2 = _bias_call(rpb[l])
        w_in_bf = w_in[l].astype(BF16)
        wpa_bf, wpb_bf, wo_bf = w_pa[l].astype(BF16), w_pb[l].astype(BF16), w_o[l].astype(BF16)
        qg_t = jnp.tile(q_norm_g[l], N_HEADS).reshape(1, COL)
        kg_t = jnp.tile(k_norm_g[l], N_HEADS).reshape(1, COL)
        params = (t2, norm_g[l], w_in_bf, qg_t, kg_t, conv_w[l], conv_b[l], wpa_bf, wpb_bf, wo_bf)
        y_prompt = _stream(y_prompt, mod3, 0, *params)
        y_sample = _stream(y_sample, mod3, nbp, *params)
    return (y_prompt, y_sample)
```

```python
import functools

import jax
import jax.numpy as jnp
from jax import lax
from jax.experimental import pallas as pl
from jax.experimental.pallas import tpu as pltpu

F32 = jnp.float32
BF16 = jnp.bfloat16

GRID_W = 64
N_HEADS = 16
HEAD_DIM = 64
WIN_R = 8
WIN_C = 16
CONV_K = 3
RMS_EPS = 1e-6

LANES = 128
MXU_DIM = 256
HEADS_PER_GROUP = MXU_DIM // HEAD_DIM
N_GROUPS = N_HEADS // HEADS_PER_GROUP
VMEM_LIMIT_BYTES = 56 * 1024 * 1024

MASK_VALUE = -1e30

COL = 1024
ZB_AB, ZB_U, ZB_SZ, ZB_Q, ZB_K, ZB_V, ZB_SBZ, ZB_SGA, ZB_SGB = 0, 1, 2, 3, 4, 5, 6, 7, 9
N_ZBLOCKS = 11
PJ_AB, PJ_AC, PJ_AX, PJ_AZ, PJ_Q, PJ_K, PJ_V, PJ_BZ, PJ_G0 = 0, 1, 2, 3, 4, 5, 6, 7, 8
N_PJ = 12


def _silu(x):
    return x * jax.nn.sigmoid(x)


def _mod_kernel(c_ref, w_ref, b_ref, o_ref):
    s = _silu(c_ref[...])
    o_ref[...] = jnp.dot(s.astype(BF16), w_ref[...].astype(BF16),
                         preferred_element_type=F32) + b_ref[...]


def _mod_call(c_all, w_ada, b_ada):
    nb, d = c_all.shape
    n = w_ada.shape[1]
    tn = 1024
    return pl.pallas_call(
        _mod_kernel,
        grid=(n // tn,),
        in_specs=[
            pl.BlockSpec((nb, d), lambda j: (0, 0)),
            pl.BlockSpec((d, tn), lambda j: (0, j)),
            pl.BlockSpec((1, tn), lambda j: (0, j)),
        ],
        out_specs=pl.BlockSpec((nb, tn), lambda j: (0, j)),
        out_shape=jax.ShapeDtypeStruct((nb, n), F32),
        compiler_params=pltpu.CompilerParams(
            dimension_semantics=("arbitrary",), vmem_limit_bytes=VMEM_LIMIT_BYTES),
        name="mod",
    )(c_all, w_ada, b_ada.reshape(1, n))


def _bias_kernel(rpb_ref, o_ref):
    h = pl.program_id(0)
    shape = (GRID_W, 2 * GRID_W)
    c = lax.broadcasted_iota(jnp.int32, shape, 0)
    l = lax.broadcasted_iota(jnp.int32, shape, 1)
    upper = l >= GRID_W
    kc = jnp.where(upper, l - GRID_W, l)
    cs = jnp.clip(c - WIN_C // 2, 0, GRID_W - WIN_C)
    valid = (kc >= cs) & (kc < cs + WIN_C)
    e_idx = kc - c + (WIN_C - 1)
    for d in range(2 * WIN_R - 2):
        acc = jnp.full(shape, MASK_VALUE, F32)
        for e in range(2 * WIN_C - 1):
            val = jnp.where(upper, rpb_ref[h, d + 1, e], rpb_ref[h, d, e])
            acc = jnp.where(valid & (e_idx == e), val, acc)
        o_ref[0, d] = acc


def _bias_call(rpb):
    nh = rpb.shape[0]
    nd = 2 * WIN_R - 2
    return pl.pallas_call(
        _bias_kernel,
        grid=(nh,),
        in_specs=[pl.BlockSpec(memory_space=pltpu.SMEM)],
        out_specs=pl.BlockSpec((1, nd, GRID_W, 2 * GRID_W), lambda h: (h, 0, 0, 0)),
        out_shape=jax.ShapeDtypeStruct((nh, nd, GRID_W, 2 * GRID_W), F32),
        compiler_params=pltpu.CompilerParams(dimension_semantics=("arbitrary",)),
        name="bias",
    )(rpb)


def _head_rms(o, gain):
    r = lax.broadcasted_iota(jnp.int32, (MXU_DIM, MXU_DIM), 0) // HEAD_DIM
    c = lax.broadcasted_iota(jnp.int32, (MXU_DIM, MXU_DIM), 1) // HEAD_DIM
    bd = jnp.where(r == c, 1.0, 0.0).astype(BF16)
    outs = []
    for k in range(o.shape[1] // MXU_DIM):
        oc = o[:, k * MXU_DIM:(k + 1) * MXU_DIM]
        ss = jnp.dot((oc * oc).astype(BF16), bd, preferred_element_type=F32)
        outs.append(oc * lax.rsqrt(ss * (1.0 / HEAD_DIM) + RMS_EPS))
    return jnp.concatenate(outs, axis=1) * gain


def _proj_kernel(x_ref, g_ref, mod_ref, w_ref, gain_ref, o_ref, h_scr, ac_scr,
                 *, tm, norm_rows, dot_rows):
    j = pl.program_id(1)

    @pl.when(j == 0)
    def _():
        shift = mod_ref[0, 0:1, :]
        scl1 = 1.0 + mod_ref[0, 1:2, :]
        g = g_ref[...]

        def body(ci, carry):
            r0 = pl.multiple_of(ci * norm_rows, norm_rows)
            x = x_ref[pl.ds(r0, norm_rows), :]
            ms = jnp.mean(x * x, axis=-1, keepdims=True)
            xn = x * lax.rsqrt(ms + RMS_EPS)
            h_scr[pl.ds(r0, norm_rows), :] = ((xn * g) * scl1 + shift).astype(BF16)
            return carry

        lax.fori_loop(0, tm // norm_rows, body, 0)

    def for_row_chunks(epilogue):
        for c in range(tm // dot_rows):
            rows = slice(c * dot_rows, (c + 1) * dot_rows)
            o = jnp.dot(h_scr[rows, :], w_ref[...], preferred_element_type=F32)
            epilogue(rows, o)

    @pl.when(j == PJ_AC)
    def _():
        def epilogue(rows, o):
            ac_scr[rows, :] = o
        for_row_chunks(epilogue)

    @pl.when(j == PJ_AX)
    def _():
        def epilogue(rows, o):
            o_ref[rows, :] = (ac_scr[rows, :] * o).astype(BF16)
        for_row_chunks(epilogue)

    @pl.when((j == PJ_Q) | (j == PJ_K))
    def _():
        def epilogue(rows, o):
            o_ref[rows, :] = _head_rms(o, gain_ref[0]).astype(BF16)
        for_row_chunks(epilogue)

    is_silu = (j == PJ_AZ) | (j == PJ_BZ)
    is_sigmoid = j >= PJ_G0

    @pl.when((j == PJ_AB) | (j == PJ_V) | is_silu | is_sigmoid)
    def _():
        def epilogue(rows, o):
            sg = jax.nn.sigmoid(o)
            o_ref[rows, :] = jnp.where(is_sigmoid, sg, jnp.where(is_silu, o * sg, o)).astype(BF16)
        for_row_chunks(epilogue)


def _proj_call(x2d, norm_g, mod3, w_in_bf, qk_gain, *, batch_of_tile, tm):
    t, d = x2d.shape
    n = w_in_bf.shape[1]
    assert n == N_PJ * COL and t % tm == 0
    kern = functools.partial(_proj_kernel, tm=tm, norm_rows=128, dot_rows=256)
    return pl.pallas_call(
        kern,
        grid=(t // tm, N_PJ),
        in_specs=[
            pl.BlockSpec((tm, d), lambda i, j: (i, 0)),
            pl.BlockSpec((1, d), lambda i, j: (0, 0)),
            pl.BlockSpec((1, 3, d), lambda i, j: (batch_of_tile(i), 0, 0)),
            pl.BlockSpec((d, COL), lambda i, j: (0, j)),
            pl.BlockSpec((1, 1, COL), lambda i, j: (jnp.where(j == PJ_K, 1, 0), 0, 0)),
        ],
        out_specs=pl.BlockSpec((tm, COL), lambda i, j: (i, jnp.where(j >= PJ_AX, j - 1, j))),
        out_shape=jax.ShapeDtypeStruct((t, N_ZBLOCKS * COL), BF16),
        scratch_shapes=[pltpu.VMEM((tm, d), BF16), pltpu.VMEM((tm, COL), F32)],
        compiler_params=pltpu.CompilerParams(
            dimension_semantics=("arbitrary", "arbitrary"),
            vmem_limit_bytes=VMEM_LIMIT_BYTES),
        name="proj",
    )(x2d, norm_g.reshape(1, d), mod3, w_in_bf, qk_gain)


def _attn_kernel(q_ref, kp_ref, kc_ref, kn_ref, vp_ref, vc_ref, vn_ref, bz_ref, t2_ref,
                 o_ref, kbuf, vbuf, *, rb, rows):
    i = pl.program_id(1)
    tb = rb * GRID_W
    kbuf[0:tb] = kp_ref[...]
    kbuf[tb:2 * tb] = kc_ref[...]
    kbuf[2 * tb:3 * tb] = kn_ref[...]
    vbuf[0:tb] = vp_ref[...]
    vbuf[tb:2 * tb] = vc_ref[...]
    vbuf[2 * tb:3 * tb] = vn_ref[...]

    lane_head = lax.broadcasted_iota(jnp.int32, (GRID_W, MXU_DIM), 1) // HEAD_DIM
    nk = WIN_R * GRID_W

    def row_body(jr, carry):
        r = i * rb + jr
        rs = jnp.clip(r - WIN_R // 2, 0, rows - WIN_R)
        start = pl.multiple_of((rs - (i - 1) * rb) * GRID_W, GRID_W)
        dbase = rs - r + (WIN_R - 1)
        q_off = pl.multiple_of(jr * GRID_W, GRID_W)
        for g in range(N_GROUPS):
            cs = slice(g * MXU_DIM, (g + 1) * MXU_DIM)
            qg = q_ref[pl.ds(q_off, GRID_W), cs]
            zero = jnp.zeros_like(qg)
            q4 = jnp.concatenate(
                [jnp.where(lane_head == h, qg, zero) for h in range(HEADS_PER_GROUP)], axis=0)
            kw = kbuf[pl.ds(start, nk), cs]
            s = lax.dot_general(q4, kw, (((1,), (1,)), ((), ())),
                                preferred_element_type=F32)
            cols = []
            for cb in range(nk // LANES):
                bias = jnp.concatenate(
                    [t2_ref[g * HEADS_PER_GROUP + h, dbase + 2 * cb]
                     for h in range(HEADS_PER_GROUP)], axis=0)
                cols.append(s[:, cb * LANES:(cb + 1) * LANES] + bias)
            s = jnp.concatenate(cols, axis=1)
            m = jnp.max(s, axis=-1, keepdims=True)
            p = jnp.exp(s - m)
            den = jnp.sum(p, axis=-1, keepdims=True)
            vw = vbuf[pl.ds(start, nk), cs]
            pv = jnp.dot(p.astype(BF16), vw, preferred_element_type=F32)
            pv = pv * (1.0 / den)
            o = jnp.zeros((GRID_W, MXU_DIM), F32)
            for h in range(HEADS_PER_GROUP):
                o = o + jnp.where(lane_head == h, pv[h * GRID_W:(h + 1) * GRID_W], 0.0)
            sbz = bz_ref[pl.ds(q_off, GRID_W), cs].astype(F32)
            o_ref[pl.ds(q_off, GRID_W), cs] = (o * sbz).astype(BF16)
        return carry

    lax.fori_loop(0, rb, row_body, 0)


def _attn_call(z, t2, *, n_batch, seq, rb):
    rows = seq // GRID_W
    nblk = rows // rb
    tb = rb * GRID_W
    assert rows % rb == 0 and rb >= WIN_R and nblk >= 2
    kern = functools.partial(_attn_kernel, rb=rb, rows=rows)

    def cur(col):
        return pl.BlockSpec((tb, COL), lambda b, i: (b * nblk + i, col))

    def prev(col):
        return pl.BlockSpec((tb, COL), lambda b, i: (b * nblk + jnp.maximum(i - 1, 0), col))

    def nxt(col):
        return pl.BlockSpec((tb, COL), lambda b, i: (b * nblk + jnp.minimum(i + 1, nblk - 1), col))

    return pl.pallas_call(
        kern,
        grid=(n_batch, nblk),
        in_specs=[cur(ZB_Q), prev(ZB_K), cur(ZB_K), nxt(ZB_K), prev(ZB_V), cur(ZB_V), nxt(ZB_V),
                  cur(ZB_SBZ),
                  pl.BlockSpec(t2.shape, lambda b, i: (0, 0, 0, 0))],
        out_specs=pl.BlockSpec((tb, COL), lambda b, i: (b * nblk + i, 0)),
        out_shape=jax.ShapeDtypeStruct((n_batch * seq, COL), BF16),
        scratch_shapes=[pltpu.VMEM((3 * tb, COL), BF16), pltpu.VMEM((3 * tb, COL), BF16)],
        compiler_params=pltpu.CompilerParams(
            dimension_semantics=("arbitrary", "arbitrary"),
            vmem_limit_bytes=VMEM_LIMIT_BYTES),
        name="attn",
    )(z, z, z, z, z, z, z, z, t2)


def _out_kernel(ab_ref, u_ref, up_ref, un_ref, sz_ref, yb_ref, sga0_ref, sga1_ref, sgb0_ref,
                sgb1_ref, x_ref, mod_ref, cw_ref, cb_ref, wpa_ref, wpb_ref, wo_ref, o_ref,
                *, tm, tiles_per_seq, halo):
    t = pl.program_id(0) % tiles_per_seq
    u = u_ref[...].astype(F32)
    prev_row = jnp.where(t == 0, 0.0, up_ref[...].astype(F32)[halo - 1:halo, :])
    next_row = jnp.where(t == tiles_per_seq - 1, 0.0, un_ref[...].astype(F32)[0:1, :])
    row = lax.broadcasted_iota(jnp.int32, u.shape, 0)
    u_m1 = jnp.where(row == 0, prev_row, pltpu.roll(u, 1, 0))
    u_p1 = jnp.where(row == tm - 1, next_row, pltpu.roll(u, tm - 1, 0))
    conv = u_m1 * cw_ref[0:1, :] + u * cw_ref[1:2, :] + u_p1 * cw_ref[2:3, :] + cb_ref[...]
    ya = ab_ref[...].astype(F32) * conv * sz_ref[...].astype(F32)
    pa = jnp.dot(ya.astype(BF16), wpa_ref[...], preferred_element_type=F32)
    pb = jnp.dot(yb_ref[...], wpb_ref[...], preferred_element_type=F32)
    m0 = (sga0_ref[...].astype(F32) * pa[:, :COL] + sgb0_ref[...].astype(F32) * pb[:, :COL])
    m1 = (sga1_ref[...].astype(F32) * pa[:, COL:] + sgb1_ref[...].astype(F32) * pb[:, COL:])
    m = jnp.concatenate([m0.astype(BF16), m1.astype(BF16)], axis=1)
    o = jnp.dot(m, wo_ref[...], preferred_element_type=F32)
    o_ref[...] = x_ref[...] + mod_ref[0, 2:3, :] * o


def _out_call(z, yb, x2d, mod3, conv_w, conv_b, wpa_bf, wpb_bf, wo_bf, *, batch_of_tile,
              seq, tm):
    t, d = x2d.shape
    dc = wpa_bf.shape[0]
    halo = 16
    tiles_per_seq = seq // tm
    ntile = t // tm
    hb = tm // halo
    nhalo = t // halo
    kern = functools.partial(_out_kernel, tm=tm, tiles_per_seq=tiles_per_seq, halo=halo)

    def zcol(col):
        return pl.BlockSpec((tm, COL), lambda i: (i, col))

    const = lambda shape: pl.BlockSpec(shape, lambda i: (0,) * len(shape),
                                       pipeline_mode=pl.Buffered(1))
    return pl.pallas_call(
        kern,
        grid=(ntile,),
        in_specs=[
            zcol(ZB_AB), zcol(ZB_U),
            pl.BlockSpec((halo, COL), lambda i: (jnp.maximum(i * hb - 1, 0), ZB_U)),
            pl.BlockSpec((halo, COL), lambda i: (jnp.minimum((i + 1) * hb, nhalo - 1), ZB_U)),
            zcol(ZB_SZ),
            pl.BlockSpec((tm, COL), lambda i: (i, 0)),
            zcol(ZB_SGA), zcol(ZB_SGA + 1), zcol(ZB_SGB), zcol(ZB_SGB + 1),
            pl.BlockSpec((tm, d), lambda i: (i, 0)),
            pl.BlockSpec((1, 3, d), lambda i: (batch_of_tile(i), 0, 0)),
            const((CONV_K, dc)), const((1, dc)),
            const((dc, d)), const((dc, d)), const((d, d)),
        ],
        out_specs=pl.BlockSpec((tm, d), lambda i: (i, 0)),
        out_shape=jax.ShapeDtypeStruct((t, d), F32),
        compiler_params=pltpu.CompilerParams(
            dimension_semantics=("arbitrary",), vmem_limit_bytes=VMEM_LIMIT_BYTES),
        name="out",
    )(z, z, z, z, z, yb, z, z, z, z, x2d, mod3, conv_w, conv_b.reshape(1, dc),
      wpa_bf, wpb_bf, wo_bf)


def _stream(x, mod3, batch0, t2, norm_g, w_in_bf, qk_gain, conv_w, conv_b, wpa_bf, wpb_bf,
            wo_bf):
    nb, seq, d = x.shape
    x2d = x.reshape(nb * seq, d)
    tm_proj, tm_out, rb = 1024, 256, 8
    z = _proj_call(x2d, norm_g, mod3, w_in_bf, qk_gain, tm=tm_proj,
                   batch_of_tile=lambda i: batch0 + i // (seq // tm_proj))
    yb = _attn_call(z, t2, n_batch=nb, seq=seq, rb=rb)
    y = _out_call(z, yb, x2d, mod3, conv_w, conv_b, wpa_bf, wpb_bf, wo_bf, seq=seq, tm=tm_out,
                  batch_of_tile=lambda i: batch0 + i // (seq // tm_out))
    return y.reshape(nb, seq, d)


def kernel(x_prompt, x_sample, c_prompt, c_sample, norm_g, w_ada, b_ada, w_in, conv_w, conv_b,
           q_norm_g, k_norm_g, rpb, w_pa, w_pb, w_o):
    depth = w_in.shape[0]
    d = x_prompt.shape[-1]
    nbp = c_prompt.shape[0]
    nbs = c_sample.shape[0]
    c_all = jnp.concatenate([c_prompt, c_sample], axis=0)
    pad = (-c_all.shape[0]) % 8
    c_all = jnp.pad(c_all, ((0, pad), (0, 0)))
    y_prompt, y_sample = x_prompt, x_sample
    for l in range(depth):
        mod3 = _mod_call(c_all, w_ada[l], b_ada[l]).reshape(c_all.shape[0], 3, d)
        t2 = _bias_call(rpb[l])
        w_in_bf = w_in[l].astype(BF16)
        wpa_bf, wpb_bf, wo_bf = w_pa[l].astype(BF16), w_pb[l].astype(BF16), w_o[l].astype(BF16)
        qk_gain = jnp.stack([jnp.tile(q_norm_g[l] * HEAD_DIM ** -0.5, N_HEADS),
                             jnp.tile(k_norm_g[l], N_HEADS)]).reshape(2, 1, COL)
        params = (t2, norm_g[l], w_in_bf, qk_gain, conv_w[l], conv_b[l], wpa_bf, wpb_bf, wo_bf)
        y_prompt = _stream(y_prompt, mod3, 0, *params)
        y_sample = _stream(y_sample, mod3, nbp, *params)
    return (y_prompt, y_sample)
```

```python
import functools

import jax
import jax.numpy as jnp
from jax import lax
from jax.experimental import pallas as pl
from jax.experimental.pallas import tpu as pltpu

F32 = jnp.float32
BF16 = jnp.bfloat16

GRID_W = 64
N_HEADS = 16
HEAD_DIM = 64
WIN_R = 8
WIN_C = 16
CONV_K = 3
RMS_EPS = 1e-6

LANES = 128
MXU_DIM = 256
HEADS_PER_GROUP = MXU_DIM // HEAD_DIM
N_GROUPS = N_HEADS // HEADS_PER_GROUP
VMEM_LIMIT_BYTES = 56 * 1024 * 1024

MASK_VALUE = -1e30
LOG2_E = 1.4426950408889634

COL = 1024
ZB_AB, ZB_U, ZB_SZ, ZB_Q, ZB_K, ZB_V, ZB_SBZ, ZB_SGA, ZB_SGB = 0, 1, 2, 3, 4, 5, 6, 7, 9
N_ZBLOCKS = 11
PJ_AB, PJ_AC, PJ_AX, PJ_AZ, PJ_Q, PJ_K, PJ_V, PJ_BZ, PJ_G0 = 0, 1, 2, 3, 4, 5, 6, 7, 8
N_PJ = 12


def _silu(x):
    return x * jax.nn.sigmoid(x)


def _mod_kernel(c_ref, w_ref, b_ref, o_ref):
    s = _silu(c_ref[...])
    o_ref[...] = jnp.dot(s.astype(BF16), w_ref[...].astype(BF16),
                         preferred_element_type=F32) + b_ref[...]


def _mod_call(c_all, w_ada, b_ada):
    nb, d = c_all.shape
    n = w_ada.shape[1]
    tn = 1024
    return pl.pallas_call(
        _mod_kernel,
        grid=(n // tn,),
        in_specs=[
            pl.BlockSpec((nb, d), lambda j: (0, 0)),
            pl.BlockSpec((d, tn), lambda j: (0, j)),
            pl.BlockSpec((1, tn), lambda j: (0, j)),
        ],
        out_specs=pl.BlockSpec((nb, tn), lambda j: (0, j)),
        out_shape=jax.ShapeDtypeStruct((nb, n), F32),
        compiler_params=pltpu.CompilerParams(
            dimension_semantics=("arbitrary",), vmem_limit_bytes=VMEM_LIMIT_BYTES),
        name="mod",
    )(c_all, w_ada, b_ada.reshape(1, n))


def _bias_kernel(rpb_ref, o_ref):
    h = pl.program_id(0)
    shape = (GRID_W, 2 * GRID_W)
    c = lax.broadcasted_iota(jnp.int32, shape, 0)
    l = lax.broadcasted_iota(jnp.int32, shape, 1)
    upper = l >= GRID_W
    kc = jnp.where(upper, l - GRID_W, l)
    cs = jnp.clip(c - WIN_C // 2, 0, GRID_W - WIN_C)
    valid = (kc >= cs) & (kc < cs + WIN_C)
    e_idx = kc - c + (WIN_C - 1)
    for d in range(2 * WIN_R - 2):
        acc = jnp.full(shape, MASK_VALUE, F32)
        for e in range(2 * WIN_C - 1):
            val = jnp.where(upper, rpb_ref[h, d + 1, e], rpb_ref[h, d, e]) * LOG2_E
            acc = jnp.where(valid & (e_idx == e), val, acc)
        o_ref[0, d] = acc


def _bias_call(rpb):
    nh = rpb.shape[0]
    nd = 2 * WIN_R - 2
    return pl.pallas_call(
        _bias_kernel,
        grid=(nh,),
        in_specs=[pl.BlockSpec(memory_space=pltpu.SMEM)],
        out_specs=pl.BlockSpec((1, nd, GRID_W, 2 * GRID_W), lambda h: (h, 0, 0, 0)),
        out_shape=jax.ShapeDtypeStruct((nh, nd, GRID_W, 2 * GRID_W), F32),
        compiler_params=pltpu.CompilerParams(dimension_semantics=("arbitrary",)),
        name="bias",
    )(rpb)


def _head_rms(o, gain):
    r = lax.broadcasted_iota(jnp.int32, (MXU_DIM, MXU_DIM), 0) // HEAD_DIM
    c = lax.broadcasted_iota(jnp.int32, (MXU_DIM, MXU_DIM), 1) // HEAD_DIM
    bd = jnp.where(r == c, 1.0, 0.0).astype(BF16)
    outs = []
    for k in range(o.shape[1] // MXU_DIM):
        oc = o[:, k * MXU_DIM:(k + 1) * MXU_DIM]
        ss = jnp.dot((oc * oc).astype(BF16), bd, preferred_element_type=F32)
        outs.append(oc * lax.rsqrt(ss * (1.0 / HEAD_DIM) + RMS_EPS))
    return jnp.concatenate(outs, axis=1) * gain


def _proj_kernel(x_ref, g_ref, mod_ref, w_ref, gain_ref, o_ref, h_scr, ac_scr,
                 *, tm, norm_rows, dot_rows):
    j = pl.program_id(1)

    @pl.when(j == 0)
    def _():
        shift = mod_ref[0, 0:1, :]
        scl1 = 1.0 + mod_ref[0, 1:2, :]
        g = g_ref[...]

        def body(ci, carry):
            r0 = pl.multiple_of(ci * norm_rows, norm_rows)
            x = x_ref[pl.ds(r0, norm_rows), :]
            ms = jnp.mean(x * x, axis=-1, keepdims=True)
            xn = x * lax.rsqrt(ms + RMS_EPS)
            h_scr[pl.ds(r0, norm_rows), :] = ((xn * g) * scl1 + shift).astype(BF16)
            return carry

        lax.fori_loop(0, tm // norm_rows, body, 0)

    def for_row_chunks(epilogue):
        for c in range(tm // dot_rows):
            rows = slice(c * dot_rows, (c + 1) * dot_rows)
            o = jnp.dot(h_scr[rows, :], w_ref[...], preferred_element_type=F32)
            epilogue(rows, o)

    @pl.when(j == PJ_AC)
    def _():
        def epilogue(rows, o):
            ac_scr[rows, :] = o
        for_row_chunks(epilogue)

    @pl.when(j == PJ_AX)
    def _():
        def epilogue(rows, o):
            o_ref[rows, :] = (ac_scr[rows, :] * o).astype(BF16)
        for_row_chunks(epilogue)

    @pl.when((j == PJ_Q) | (j == PJ_K))
    def _():
        def epilogue(rows, o):
            o_ref[rows, :] = _head_rms(o, gain_ref[0]).astype(BF16)
        for_row_chunks(epilogue)

    is_silu = (j == PJ_AZ) | (j == PJ_BZ)
    is_sigmoid = j >= PJ_G0

    @pl.when((j == PJ_AB) | (j == PJ_V) | is_silu | is_sigmoid)
    def _():
        def epilogue(rows, o):
            sg = jax.nn.sigmoid(o)
            o_ref[rows, :] = jnp.where(is_sigmoid, sg, jnp.where(is_silu, o * sg, o)).astype(BF16)
        for_row_chunks(epilogue)


def _proj_call(x2d, norm_g, mod3, w_in_bf, qk_gain, *, batch_of_tile, tm):
    t, d = x2d.shape
    n = w_in_bf.shape[1]
    assert n == N_PJ * COL and t % tm == 0
    kern = functools.partial(_proj_kernel, tm=tm, norm_rows=128, dot_rows=256)
    return pl.pallas_call(
        kern,
        grid=(t // tm, N_PJ),
        in_specs=[
            pl.BlockSpec((tm, d), lambda i, j: (i, 0)),
            pl.BlockSpec((1, d), lambda i, j: (0, 0)),
            pl.BlockSpec((1, 3, d), lambda i, j: (batch_of_tile(i), 0, 0)),
            pl.BlockSpec((d, COL), lambda i, j: (0, j)),
            pl.BlockSpec((1, 1, COL), lambda i, j: (jnp.where(j == PJ_K, 1, 0), 0, 0)),
        ],
        out_specs=pl.BlockSpec((tm, COL), lambda i, j: (i, jnp.where(j >= PJ_AX, j - 1, j))),
        out_shape=jax.ShapeDtypeStruct((t, N_ZBLOCKS * COL), BF16),
        scratch_shapes=[pltpu.VMEM((tm, d), BF16), pltpu.VMEM((tm, COL), F32)],
        compiler_params=pltpu.CompilerParams(
            dimension_semantics=("arbitrary", "arbitrary"),
            vmem_limit_bytes=VMEM_LIMIT_BYTES),
        name="proj",
    )(x2d, norm_g.reshape(1, d), mod3, w_in_bf, qk_gain)


def _attn_kernel(q_ref, bz_ref, t2_ref, z_hbm, o_ref, kv_buf, sem, *, rb, rows, seq):
    b = pl.program_id(0)
    i = pl.program_id(1)
    nblk = pl.num_programs(1)
    step = b * nblk + i
    slot = step % 2
    win_rows = rb + WIN_R
    nk = WIN_R * GRID_W

    def window_row0(blk):
        return jnp.clip(blk * rb - WIN_R // 2, 0, rows - win_rows)

    def kv_copy(bb, blk, sl):
        tok0 = pl.multiple_of(bb * seq + window_row0(blk) * GRID_W, GRID_W)
        return pltpu.make_async_copy(
            z_hbm.at[pl.ds(tok0, win_rows * GRID_W), pl.ds(ZB_K * COL, 2 * COL)],
            kv_buf.at[sl], sem.at[sl])

    @pl.when(step == 0)
    def _():
        kv_copy(b, i, slot).start()

    @pl.when(step + 1 < pl.num_programs(0) * nblk)
    def _():
        wrap = i == nblk - 1
        kv_copy(jnp.where(wrap, b + 1, b), jnp.where(wrap, 0, i + 1), 1 - slot).start()

    kv_copy(b, i, slot).wait()

    lane_head = lax.broadcasted_iota(jnp.int32, (GRID_W, MXU_DIM), 1) // HEAD_DIM
    row0 = window_row0(i)

    def row_body(jr, carry):
        r = i * rb + jr
        rs = jnp.clip(r - WIN_R // 2, 0, rows - WIN_R)
        start = pl.multiple_of((rs - row0) * GRID_W, GRID_W)
        dbase = rs - r + (WIN_R - 1)
        q_off = pl.multiple_of(jr * GRID_W, GRID_W)
        for g in range(N_GROUPS):
            cs = slice(g * MXU_DIM, (g + 1) * MXU_DIM)
            vs = slice(COL + g * MXU_DIM, COL + (g + 1) * MXU_DIM)
            qg = q_ref[pl.ds(q_off, GRID_W), cs]
            zero = jnp.zeros_like(qg)
            q4 = jnp.concatenate(
                [jnp.where(lane_head == h, qg, zero) for h in range(HEADS_PER_GROUP)], axis=0)
            kw = kv_buf[slot, pl.ds(start, nk), cs]
            s = lax.dot_general(q4, kw, (((1,), (1,)), ((), ())),
                                preferred_element_type=F32)
            cols = []
            for cb in range(nk // LANES):
                bias = jnp.concatenate(
                    [t2_ref[g * HEADS_PER_GROUP + h, dbase + 2 * cb]
                     for h in range(HEADS_PER_GROUP)], axis=0)
                cols.append(s[:, cb * LANES:(cb + 1) * LANES] + bias)
            s = jnp.concatenate(cols, axis=1)
            m = jnp.max(s, axis=-1, keepdims=True)
            p = jnp.exp2(s - m)
            den = jnp.sum(p, axis=-1, keepdims=True)
            vw = kv_buf[slot, pl.ds(start, nk), vs]
            pv = jnp.dot(p.astype(BF16), vw, preferred_element_type=F32)
            pv = pv * (1.0 / den)
            o = jnp.zeros((GRID_W, MXU_DIM), F32)
            for h in range(HEADS_PER_GROUP):
                o = o + jnp.where(lane_head == h, pv[h * GRID_W:(h + 1) * GRID_W], 0.0)
            sbz = bz_ref[pl.ds(q_off, GRID_W), cs].astype(F32)
            o_ref[pl.ds(q_off, GRID_W), cs] = (o * sbz).astype(BF16)
        return carry

    lax.fori_loop(0, rb, row_body, 0, unroll=4)


def _attn_call(z, t2, *, n_batch, seq, rb):
    rows = seq // GRID_W
    nblk = rows // rb
    tb = rb * GRID_W
    win_tokens = (rb + WIN_R) * GRID_W
    assert rows % rb == 0 and rows >= rb + WIN_R
    kern = functools.partial(_attn_kernel, rb=rb, rows=rows, seq=seq)

    def cur(col):
        return pl.BlockSpec((tb, COL), lambda b, i: (b * nblk + i, col))

    return pl.pallas_call(
        kern,
        grid=(n_batch, nblk),
        in_specs=[cur(ZB_Q), cur(ZB_SBZ),
                  pl.BlockSpec(t2.shape, lambda b, i: (0, 0, 0, 0)),
                  pl.BlockSpec(memory_space=pl.ANY)],
        out_specs=pl.BlockSpec((tb, COL), lambda b, i: (b * nblk + i, 0)),
        out_shape=jax.ShapeDtypeStruct((n_batch * seq, COL), BF16),
        scratch_shapes=[pltpu.VMEM((2, win_tokens, 2 * COL), BF16),
                        pltpu.SemaphoreType.DMA((2,))],
        compiler_params=pltpu.CompilerParams(
            dimension_semantics=("arbitrary", "arbitrary"),
            vmem_limit_bytes=VMEM_LIMIT_BYTES),
        name="attn",
    )(z, z, t2, z)


def _out_kernel(ab_ref, u_ref, up_ref, un_ref, sz_ref, yb_ref, sga0_ref, sga1_ref, sgb0_ref,
                sgb1_ref, x_ref, mod_ref, cw_ref, cb_ref, wpa_ref, wpb_ref, wo_ref, o_ref,
                *, tm, tiles_per_seq, halo):
    t = pl.program_id(0) % tiles_per_seq
    u = u_ref[...].astype(F32)
    prev_row = jnp.where(t == 0, 0.0, up_ref[...].astype(F32)[halo - 1:halo, :])
    next_row = jnp.where(t == tiles_per_seq - 1, 0.0, un_ref[...].astype(F32)[0:1, :])
    row = lax.broadcasted_iota(jnp.int32, u.shape, 0)
    u_m1 = jnp.where(row == 0, prev_row, pltpu.roll(u, 1, 0))
    u_p1 = jnp.where(row == tm - 1, next_row, pltpu.roll(u, tm - 1, 0))
    conv = u_m1 * cw_ref[0:1, :] + u * cw_ref[1:2, :] + u_p1 * cw_ref[2:3, :] + cb_ref[...]
    ya = ab_ref[...].astype(F32) * conv * sz_ref[...].astype(F32)
    pa = jnp.dot(ya.astype(BF16), wpa_ref[...], preferred_element_type=F32)
    pb = jnp.dot(yb_ref[...], wpb_ref[...], preferred_element_type=F32)
    m0 = (sga0_ref[...].astype(F32) * pa[:, :COL] + sgb0_ref[...].astype(F32) * pb[:, :COL])
    m1 = (sga1_ref[...].astype(F32) * pa[:, COL:] + sgb1_ref[...].astype(F32) * pb[:, COL:])
    m = jnp.concatenate([m0.astype(BF16), m1.astype(BF16)], axis=1)
    o = jnp.dot(m, wo_ref[...], preferred_element_type=F32)
    o_ref[...] = x_ref[...] + mod_ref[0, 2:3, :] * o


def _out_call(z, yb, x2d, mod3, conv_w, conv_b, wpa_bf, wpb_bf, wo_bf, *, batch_of_tile,
              seq, tm):
    t, d = x2d.shape
    dc = wpa_bf.shape[0]
    halo = 16
    tiles_per_seq = seq // tm
    ntile = t // tm
    hb = tm // halo
    nhalo = t // halo
    kern = functools.partial(_out_kernel, tm=tm, tiles_per_seq=tiles_per_seq, halo=halo)

    def zcol(col):
        return pl.BlockSpec((tm, COL), lambda i: (i, col))

    const = lambda shape: pl.BlockSpec(shape, lambda i: (0,) * len(shape),
                                       pipeline_mode=pl.Buffered(1))
    return pl.pallas_call(
        kern,
        grid=(ntile,),
        in_specs=[
            zcol(ZB_AB), zcol(ZB_U),
            pl.BlockSpec((halo, COL), lambda i: (jnp.maximum(i * hb - 1, 0), ZB_U)),
            pl.BlockSpec((halo, COL), lambda i: (jnp.minimum((i + 1) * hb, nhalo - 1), ZB_U)),
            zcol(ZB_SZ),
            pl.BlockSpec((tm, COL), lambda i: (i, 0)),
            zcol(ZB_SGA), zcol(ZB_SGA + 1), zcol(ZB_SGB), zcol(ZB_SGB + 1),
            pl.BlockSpec((tm, d), lambda i: (i, 0)),
            pl.BlockSpec((1, 3, d), lambda i: (batch_of_tile(i), 0, 0)),
            const((CONV_K, dc)), const((1, dc)),
            const((dc, d)), const((dc, d)), const((d, d)),
        ],
        out_specs=pl.BlockSpec((tm, d), lambda i: (i, 0)),
        out_shape=jax.ShapeDtypeStruct((t, d), F32),
        compiler_params=pltpu.CompilerParams(
            dimension_semantics=("arbitrary",), vmem_limit_bytes=VMEM_LIMIT_BYTES),
        name="out",
    )(z, z, z, z, z, yb, z, z, z, z, x2d, mod3, conv_w, conv_b.reshape(1, dc),
      wpa_bf, wpb_bf, wo_bf)


def _stream(x, mod3, batch0, t2, norm_g, w_in_bf, qk_gain, conv_w, conv_b, wpa_bf, wpb_bf,
            wo_bf):
    nb, seq, d = x.shape
    x2d = x.reshape(nb * seq, d)
    tm_proj, tm_out, rb = 1024, 256, 8
    z = _proj_call(x2d, norm_g, mod3, w_in_bf, qk_gain, tm=tm_proj,
                   batch_of_tile=lambda i: batch0 + i // (seq // tm_proj))
    yb = _attn_call(z, t2, n_batch=nb, seq=seq, rb=rb)
    y = _out_call(z, yb, x2d, mod3, conv_w, conv_b, wpa_bf, wpb_bf, wo_bf, seq=seq, tm=tm_out,
                  batch_of_tile=lambda i: batch0 + i // (seq // tm_out))
    return y.reshape(nb, seq, d)


def kernel(x_prompt, x_sample, c_prompt, c_sample, norm_g, w_ada, b_ada, w_in, conv_w, conv_b,
           q_norm_g, k_norm_g, rpb, w_pa, w_pb, w_o):
    depth = w_in.shape[0]
    d = x_prompt.shape[-1]
    nbp = c_prompt.shape[0]
    nbs = c_sample.shape[0]
    c_all = jnp.concatenate([c_prompt, c_sample], axis=0)
    pad = (-c_all.shape[0]) % 8
    c_all = jnp.pad(c_all, ((0, pad), (0, 0)))
    y_prompt, y_sample = x_prompt, x_sample
    for l in range(depth):
        mod3 = _mod_call(c_all, w_ada[l], b_ada[l]).reshape(c_all.shape[0], 3, d)
        t2 = _bias_call(rpb[l])
        w_in_bf = w_in[l].astype(BF16)
        wpa_bf, wpb_bf, wo_bf = w_pa[l].astype(BF16), w_pb[l].astype(BF16), w_o[l].astype(BF16)
        qk_gain = jnp.stack([jnp.tile(q_norm_g[l] * (HEAD_DIM ** -0.5 * LOG2_E), N_HEADS),
                             jnp.tile(k_norm_g[l], N_HEADS)]).reshape(2, 1, COL)
        params = (t2, norm_g[l], w_in_bf, qk_gain, conv_w[l], conv_b[l], wpa_bf, wpb_bf, wo_bf)
        y_prompt = _stream(y_prompt, mod3, 0, *params)
        y_sample = _stream(y_sample, mod3, nbp, *params)
    return (y_prompt, y_sample)
```

```python
import functools

import jax
import jax.numpy as jnp
from jax import lax
from jax.experimental import pallas as pl
from jax.experimental.pallas import tpu as pltpu

F32 = jnp.float32
BF16 = jnp.bfloat16

GRID_W = 64
N_HEADS = 16
HEAD_DIM = 64
WIN_R = 8
WIN_C = 16
CONV_K = 3
RMS_EPS = 1e-6

LANES = 128
MXU_DIM = 256
HEADS_PER_GROUP = MXU_DIM // HEAD_DIM
N_GROUPS = N_HEADS // HEADS_PER_GROUP
VMEM_LIMIT_BYTES = 56 * 1024 * 1024

MASK_VALUE = -1e30
LOG2_E = 1.4426950408889634

COL = 1024
ZB_AB, ZB_SZ, ZB_Q, ZB_SBZ, ZB_K, ZB_V, ZB_SGA, ZB_SGB = 0, 1, 2, 3, 4, 5, 6, 8
N_ZBLOCKS = 10
PJ_W = 2 * COL
PJ_COLUMN_ORDER = (1, 2, 0, 3, 4, 7, 5, 6, 8, 9, 10, 11)
PJ_U, PJ_AB_AZ, PJ_Q_BZ, PJ_K_V, PJ_GA, PJ_GB = 0, 1, 2, 3, 4, 5
N_PJ = 6


def _silu(x):
    return x * jax.nn.sigmoid(x)


def _mod_kernel(c_ref, w_ref, b_ref, o_ref):
    s = _silu(c_ref[...])
    o_ref[...] = jnp.dot(s.astype(BF16), w_ref[...].astype(BF16),
                         preferred_element_type=F32) + b_ref[...]


def _mod_call(c_all, w_ada, b_ada):
    nb, d = c_all.shape
    n = w_ada.shape[1]
    tn = 1024
    return pl.pallas_call(
        _mod_kernel,
        grid=(n // tn,),
        in_specs=[
            pl.BlockSpec((nb, d), lambda j: (0, 0)),
            pl.BlockSpec((d, tn), lambda j: (0, j)),
            pl.BlockSpec((1, tn), lambda j: (0, j)),
        ],
        out_specs=pl.BlockSpec((nb, tn), lambda j: (0, j)),
        out_shape=jax.ShapeDtypeStruct((nb, n), F32),
        compiler_params=pltpu.CompilerParams(
            dimension_semantics=("arbitrary",), vmem_limit_bytes=VMEM_LIMIT_BYTES),
        name="mod",
    )(c_all, w_ada, b_ada.reshape(1, n))


def _bias_kernel(rpb_ref, o_ref):
    h = pl.program_id(0)
    shape = (GRID_W, 2 * GRID_W)
    c = lax.broadcasted_iota(jnp.int32, shape, 0)
    l = lax.broadcasted_iota(jnp.int32, shape, 1)
    upper = l >= GRID_W
    kc = jnp.where(upper, l - GRID_W, l)
    cs = jnp.clip(c - WIN_C // 2, 0, GRID_W - WIN_C)
    valid = (kc >= cs) & (kc < cs + WIN_C)
    e_idx = kc - c + (WIN_C - 1)
    for d in range(2 * WIN_R - 2):
        acc = jnp.full(shape, MASK_VALUE, F32)
        for e in range(2 * WIN_C - 1):
            val = jnp.where(upper, rpb_ref[h, d + 1, e], rpb_ref[h, d, e]) * LOG2_E
            acc = jnp.where(valid & (e_idx == e), val, acc)
        o_ref[0, d] = acc


def _bias_call(rpb):
    nh = rpb.shape[0]
    nd = 2 * WIN_R - 2
    return pl.pallas_call(
        _bias_kernel,
        grid=(nh,),
        in_specs=[pl.BlockSpec(memory_space=pltpu.SMEM)],
        out_specs=pl.BlockSpec((1, nd, GRID_W, 2 * GRID_W), lambda h: (h, 0, 0, 0)),
        out_shape=jax.ShapeDtypeStruct((nh, nd, GRID_W, 2 * GRID_W), F32),
        compiler_params=pltpu.CompilerParams(dimension_semantics=("arbitrary",)),
        name="bias",
    )(rpb)


def _head_rms(o, gain):
    r = lax.broadcasted_iota(jnp.int32, (MXU_DIM, MXU_DIM), 0) // HEAD_DIM
    c = lax.broadcasted_iota(jnp.int32, (MXU_DIM, MXU_DIM), 1) // HEAD_DIM
    bd = jnp.where(r == c, 1.0, 0.0).astype(BF16)
    outs = []
    for k in range(o.shape[1] // MXU_DIM):
        oc = o[:, k * MXU_DIM:(k + 1) * MXU_DIM]
        ss = jnp.dot((oc * oc).astype(BF16), bd, preferred_element_type=F32)
        outs.append(oc * lax.rsqrt(ss * (1.0 / HEAD_DIM) + RMS_EPS))
    return jnp.concatenate(outs, axis=1) * gain


def _proj_kernel(x_ref, g_ref, mod_ref, w_ref, gain_ref, u_ref, z_ref, h_scr,
                 *, tm, n_tiles, norm_rows, dot_rows):
    warm = tm // norm_rows
    s = pl.program_id(0)
    i = jnp.maximum(s - warm, 0) // N_PJ
    j = jnp.maximum(s - warm, 0) % N_PJ
    live = s >= warm
    cur = i % 2

    def norm_chunk(slot, row0):
        x = x_ref[...]
        ms = jnp.mean(x * x, axis=-1, keepdims=True)
        xn = x * lax.rsqrt(ms + RMS_EPS)
        h = (xn * g_ref[...]) * (1.0 + mod_ref[0, 1:2, :]) + mod_ref[0, 0:1, :]
        h_scr[slot, pl.ds(row0, norm_rows), :] = h.astype(BF16)

    @pl.when(s < warm)
    def _():
        norm_chunk(0, pl.multiple_of(s * norm_rows, norm_rows))

    def for_row_chunks(epilogue):
        for c in range(tm // dot_rows):
            rows = slice(c * dot_rows, (c + 1) * dot_rows)
            hc = h_scr[cur, rows, :]
            lo = jnp.dot(hc, w_ref[:, :COL], preferred_element_type=F32)
            hi = jnp.dot(hc, w_ref[:, COL:], preferred_element_type=F32)
            epilogue(rows, lo, hi)

    def z_store(rows, lo, hi):
        z_ref[rows, :COL] = lo.astype(BF16)
        z_ref[rows, COL:] = hi.astype(BF16)

    @pl.when(live & (j == PJ_U))
    def _():
        norm_chunk(1 - cur, PJ_U * norm_rows)

        def epilogue(rows, lo, hi):
            u_ref[rows, :] = (lo * hi).astype(BF16)
        for_row_chunks(epilogue)

    @pl.when(live & (j == PJ_AB_AZ))
    def _():
        norm_chunk(1 - cur, PJ_AB_AZ * norm_rows)
        for_row_chunks(lambda rows, lo, hi: z_store(rows, lo, _silu(hi)))

    @pl.when(live & (j == PJ_Q_BZ))
    def _():
        norm_chunk(1 - cur, PJ_Q_BZ * norm_rows)
        for_row_chunks(lambda rows, lo, hi: z_store(rows, _head_rms(lo, gain_ref[0]), _silu(hi)))

    @pl.when(live & (j == PJ_K_V))
    def _():
        norm_chunk(1 - cur, PJ_K_V * norm_rows)
        for_row_chunks(lambda rows, lo, hi: z_store(rows, _head_rms(lo, gain_ref[0]), hi))

    @pl.when(live & (j >= PJ_GA))
    def _():
        for_row_chunks(
            lambda rows, lo, hi: z_store(rows, jax.nn.sigmoid(lo), jax.nn.sigmoid(hi)))


def _proj_call(x2d, norm_g, mod3, w_in_bf, qk_gain, *, batch_of_tile, tm):
    t, d = x2d.shape
    assert w_in_bf.shape[1] == N_PJ * PJ_W and t % tm == 0
    n_tiles = t // tm
    norm_rows = tm // 4
    warm = tm // norm_rows
    assert warm <= PJ_K_V + 1
    kern = functools.partial(_proj_kernel, tm=tm, n_tiles=n_tiles, norm_rows=norm_rows,
                             dot_rows=256)

    def tile_block(s):
        sl = jnp.maximum(s - warm, 0)
        return sl // N_PJ, sl % N_PJ

    def x_index(s):
        i, j = tile_block(s)
        nxt = jnp.minimum(i + 1, n_tiles - 1) * warm + jnp.minimum(j, warm - 1)
        return jnp.where(s < warm, s, nxt), 0

    def mod_index(s):
        i, _ = tile_block(s)
        nxt = batch_of_tile(jnp.minimum(i + 1, n_tiles - 1))
        return jnp.where(s < warm, batch_of_tile(0), nxt), 0, 0

    return pl.pallas_call(
        kern,
        grid=(warm + n_tiles * N_PJ,),
        in_specs=[
            pl.BlockSpec((norm_rows, d), x_index),
            pl.BlockSpec((1, d), lambda s: (0, 0)),
            pl.BlockSpec((1, 3, d), mod_index),
            pl.BlockSpec((d, PJ_W), lambda s: (0, tile_block(s)[1])),
            pl.BlockSpec((1, 1, COL), lambda s: (jnp.where(tile_block(s)[1] == PJ_K_V, 1, 0), 0, 0)),
        ],
        out_specs=[
            pl.BlockSpec((tm, COL), lambda s: (tile_block(s)[0], 0)),
            pl.BlockSpec((tm, PJ_W), lambda s: (tile_block(s)[0],
                                                jnp.maximum(tile_block(s)[1] - 1, 0))),
        ],
        out_shape=[jax.ShapeDtypeStruct((t, COL), BF16),
                   jax.ShapeDtypeStruct((t, N_ZBLOCKS * COL), BF16)],
        scratch_shapes=[pltpu.VMEM((2, tm, d), BF16)],
        compiler_params=pltpu.CompilerParams(
            dimension_semantics=("arbitrary",), vmem_limit_bytes=VMEM_LIMIT_BYTES),
        name="proj",
    )(x2d, norm_g.reshape(1, d), mod3, w_in_bf, qk_gain)


def _attn_kernel(q_ref, bz_ref, t2_ref, z_hbm, o_ref, kv_buf, sem, *, rb, rows, seq):
    b = pl.program_id(0)
    i = pl.program_id(1)
    nblk = pl.num_programs(1)
    step = b * nblk + i
    slot = step % 2
    win_rows = rb + WIN_R
    nk = WIN_R * GRID_W

    def window_row0(blk):
        return jnp.clip(blk * rb - WIN_R // 2, 0, rows - win_rows)

    def kv_copy(bb, blk, sl):
        tok0 = pl.multiple_of(bb * seq + window_row0(blk) * GRID_W, GRID_W)
        return pltpu.make_async_copy(
            z_hbm.at[pl.ds(tok0, win_rows * GRID_W), pl.ds(ZB_K * COL, 2 * COL)],
            kv_buf.at[sl], sem.at[sl])

    @pl.when(step == 0)
    def _():
        kv_copy(b, i, slot).start()

    @pl.when(step + 1 < pl.num_programs(0) * nblk)
    def _():
        wrap = i == nblk - 1
        kv_copy(jnp.where(wrap, b + 1, b), jnp.where(wrap, 0, i + 1), 1 - slot).start()

    kv_copy(b, i, slot).wait()

    lane_head = lax.broadcasted_iota(jnp.int32, (GRID_W, MXU_DIM), 1) // HEAD_DIM
    row0 = window_row0(i)

    def row_body(jr, carry):
        r = i * rb + jr
        rs = jnp.clip(r - WIN_R // 2, 0, rows - WIN_R)
        start = pl.multiple_of((rs - row0) * GRID_W, GRID_W)
        dbase = rs - r + (WIN_R - 1)
        q_off = pl.multiple_of(jr * GRID_W, GRID_W)
        for g in range(N_GROUPS):
            cs = slice(g * MXU_DIM, (g + 1) * MXU_DIM)
            vs = slice(COL + g * MXU_DIM, COL + (g + 1) * MXU_DIM)
            qg = q_ref[pl.ds(q_off, GRID_W), cs]
            zero = jnp.zeros_like(qg)
            q4 = jnp.concatenate(
                [jnp.where(lane_head == h, qg, zero) for h in range(HEADS_PER_GROUP)], axis=0)
            kw = kv_buf[slot, pl.ds(start, nk), cs]
            s = lax.dot_general(q4, kw, (((1,), (1,)), ((), ())),
                                preferred_element_type=F32)
            cols = []
            for cb in range(nk // LANES):
                bias = jnp.concatenate(
                    [t2_ref[g * HEADS_PER_GROUP + h, dbase + 2 * cb]
                     for h in range(HEADS_PER_GROUP)], axis=0)
                cols.append(s[:, cb * LANES:(cb + 1) * LANES] + bias)
            s = jnp.concatenate(cols, axis=1)
            m = jnp.max(s, axis=-1, keepdims=True)
            p = jnp.exp2(s - m)
            den = jnp.sum(p, axis=-1, keepdims=True)
            vw = kv_buf[slot, pl.ds(start, nk), vs]
            pv = jnp.dot(p.astype(BF16), vw, preferred_element_type=F32)
            pv = pv * (1.0 / den)
            o = jnp.zeros((GRID_W, MXU_DIM), F32)
            for h in range(HEADS_PER_GROUP):
                o = o + jnp.where(lane_head == h, pv[h * GRID_W:(h + 1) * GRID_W], 0.0)
            sbz = bz_ref[pl.ds(q_off, GRID_W), cs].astype(F32)
            o_ref[pl.ds(q_off, GRID_W), cs] = (o * sbz).astype(BF16)
        return carry

    lax.fori_loop(0, rb, row_body, 0, unroll=4)


def _attn_call(z, t2, *, n_batch, seq, rb):
    rows = seq // GRID_W
    nblk = rows // rb
    tb = rb * GRID_W
    win_tokens = (rb + WIN_R) * GRID_W
    assert rows % rb == 0 and rows >= rb + WIN_R
    kern = functools.partial(_attn_kernel, rb=rb, rows=rows, seq=seq)

    def cur(col):
        return pl.BlockSpec((tb, COL), lambda b, i: (b * nblk + i, col))

    return pl.pallas_call(
        kern,
        grid=(n_batch, nblk),
        in_specs=[cur(ZB_Q), cur(ZB_SBZ),
                  pl.BlockSpec(t2.shape, lambda b, i: (0, 0, 0, 0)),
                  pl.BlockSpec(memory_space=pl.ANY)],
        out_specs=pl.BlockSpec((tb, COL), lambda b, i: (b * nblk + i, 0)),
        out_shape=jax.ShapeDtypeStruct((n_batch * seq, COL), BF16),
        scratch_shapes=[pltpu.VMEM((2, win_tokens, 2 * COL), BF16),
                        pltpu.SemaphoreType.DMA((2,))],
        compiler_params=pltpu.CompilerParams(
            dimension_semantics=("arbitrary", "arbitrary"),
            vmem_limit_bytes=VMEM_LIMIT_BYTES),
        name="attn",
    )(z, z, t2, z)


def _out_kernel(ab_ref, u_ref, up_ref, un_ref, sz_ref, yb_ref, sga0_ref, sga1_ref, sgb0_ref,
                sgb1_ref, x_ref, mod_ref, cw_ref, cb_ref, wpa_ref, wpb_ref, wo_ref, o_ref,
                *, tm, tiles_per_seq, halo):
    t = pl.program_id(0) % tiles_per_seq
    u = u_ref[...].astype(F32)
    prev_row = jnp.where(t == 0, 0.0, up_ref[...].astype(F32)[halo - 1:halo, :])
    next_row = jnp.where(t == tiles_per_seq - 1, 0.0, un_ref[...].astype(F32)[0:1, :])
    row = lax.broadcasted_iota(jnp.int32, u.shape, 0)
    u_m1 = jnp.where(row == 0, prev_row, pltpu.roll(u, 1, 0))
    u_p1 = jnp.where(row == tm - 1, next_row, pltpu.roll(u, tm - 1, 0))
    conv = u_m1 * cw_ref[0:1, :] + u * cw_ref[1:2, :] + u_p1 * cw_ref[2:3, :] + cb_ref[...]
    ya = ab_ref[...].astype(F32) * conv * sz_ref[...].astype(F32)
    pa = jnp.dot(ya.astype(BF16), wpa_ref[...], preferred_element_type=F32)
    pb = jnp.dot(yb_ref[...], wpb_ref[...], preferred_element_type=F32)
    m0 = (sga0_ref[...].astype(F32) * pa[:, :COL] + sgb0_ref[...].astype(F32) * pb[:, :COL])
    m1 = (sga1_ref[...].astype(F32) * pa[:, COL:] + sgb1_ref[...].astype(F32) * pb[:, COL:])
    m = jnp.concatenate([m0.astype(BF16), m1.astype(BF16)], axis=1)
    o = jnp.dot(m, wo_ref[...], preferred_element_type=F32)
    o_ref[...] = x_ref[...] + mod_ref[0, 2:3, :] * o


def _out_call(z, u, yb, x2d, mod3, conv_w, conv_b, wpa_bf, wpb_bf, wo_bf, *, batch_of_tile,
              seq, tm):
    t, d = x2d.shape
    dc = wpa_bf.shape[0]
    halo = 16
    tiles_per_seq = seq // tm
    ntile = t // tm
    hb = tm // halo
    nhalo = t // halo
    kern = functools.partial(_out_kernel, tm=tm, tiles_per_seq=tiles_per_seq, halo=halo)

    def zcol(col):
        return pl.BlockSpec((tm, COL), lambda i: (i, col))

    const = lambda shape: pl.BlockSpec(shape, lambda i: (0,) * len(shape),
                                       pipeline_mode=pl.Buffered(1))
    return pl.pallas_call(
        kern,
        grid=(ntile,),
        in_specs=[
            zcol(ZB_AB), pl.BlockSpec((tm, COL), lambda i: (i, 0)),
            pl.BlockSpec((halo, COL), lambda i: (jnp.maximum(i * hb - 1, 0), 0)),
            pl.BlockSpec((halo, COL), lambda i: (jnp.minimum((i + 1) * hb, nhalo - 1), 0)),
            zcol(ZB_SZ),
            pl.BlockSpec((tm, COL), lambda i: (i, 0)),
            zcol(ZB_SGA), zcol(ZB_SGA + 1), zcol(ZB_SGB), zcol(ZB_SGB + 1),
            pl.BlockSpec((tm, d), lambda i: (i, 0)),
            pl.BlockSpec((1, 3, d), lambda i: (batch_of_tile(i), 0, 0)),
            const((CONV_K, dc)), const((1, dc)),
            const((dc, d)), const((dc, d)), const((d, d)),
        ],
        out_specs=pl.BlockSpec((tm, d), lambda i: (i, 0)),
        out_shape=jax.ShapeDtypeStruct((t, d), F32),
        compiler_params=pltpu.CompilerParams(
            dimension_semantics=("arbitrary",), vmem_limit_bytes=VMEM_LIMIT_BYTES),
        name="out",
    )(z, u, u, u, z, yb, z, z, z, z, x2d, mod3, conv_w, conv_b.reshape(1, dc),
      wpa_bf, wpb_bf, wo_bf)


def _stream(x, mod3, batch0, t2, norm_g, w_in_bf, qk_gain, conv_w, conv_b, wpa_bf, wpb_bf,
            wo_bf):
    nb, seq, d = x.shape
    x2d = x.reshape(nb * seq, d)
    tm_proj, tm_out, rb = 1024, 256, 8
    u, z = _proj_call(x2d, norm_g, mod3, w_in_bf, qk_gain, tm=tm_proj,
                      batch_of_tile=lambda i: batch0 + i // (seq // tm_proj))
    yb = _attn_call(z, t2, n_batch=nb, seq=seq, rb=rb)
    y = _out_call(z, u, yb, x2d, mod3, conv_w, conv_b, wpa_bf, wpb_bf, wo_bf, seq=seq, tm=tm_out,
                  batch_of_tile=lambda i: batch0 + i // (seq // tm_out))
    return y.reshape(nb, seq, d)


def kernel(x_prompt, x_sample, c_prompt, c_sample, norm_g, w_ada, b_ada, w_in, conv_w, conv_b,
           q_norm_g, k_norm_g, rpb, w_pa, w_pb, w_o):
    depth = w_in.shape[0]
    d = x_prompt.shape[-1]
    nbp = c_prompt.shape[0]
    c_all = jnp.concatenate([c_prompt, c_sample], axis=0)
    pad = (-c_all.shape[0]) % 8
    c_all = jnp.pad(c_all, ((0, pad), (0, 0)))
    y_prompt, y_sample = x_prompt, x_sample
    for l in range(depth):
        mod3 = _mod_call(c_all, w_ada[l], b_ada[l]).reshape(c_all.shape[0], 3, d)
        t2 = _bias_call(rpb[l])
        w_in_bf = jnp.concatenate(
            [w_in[l][:, c * COL:(c + 1) * COL].astype(BF16) for c in PJ_COLUMN_ORDER], axis=1)
        wpa_bf, wpb_bf, wo_bf = w_pa[l].astype(BF16), w_pb[l].astype(BF16), w_o[l].astype(BF16)
        qk_gain = jnp.stack([jnp.tile(q_norm_g[l] * (HEAD_DIM ** -0.5 * LOG2_E), N_HEADS),
                             jnp.tile(k_norm_g[l], N_HEADS)]).reshape(2, 1, COL)
        params = (t2, norm_g[l], w_in_bf, qk_gain, conv_w[l], conv_b[l], wpa_bf, wpb_bf, wo_bf)
        y_prompt = _stream(y_prompt, mod3, 0, *params)
        y_sample = _stream(y_sample, mod3, nbp, *params)
    return (y_prompt, y_sample)
```

```python
import functools

import jax
import jax.numpy as jnp
from jax import lax
from jax.experimental import pallas as pl
from jax.experimental.pallas import tpu as pltpu

F32 = jnp.float32
BF16 = jnp.bfloat16

GRID_W = 64
N_HEADS = 16
HEAD_DIM = 64
WIN_R = 8
WIN_C = 16
CONV_K = 3
RMS_EPS = 1e-6

LANES = 128
MXU_DIM = 256
HEADS_PER_GROUP = MXU_DIM // HEAD_DIM
N_GROUPS = N_HEADS // HEADS_PER_GROUP
VMEM_LIMIT_BYTES = 56 * 1024 * 1024

MASK_VALUE = -1e30
LOG2_E = 1.4426950408889634

COL = 1024
ZB_U, ZB_ABSZ, ZB_Q, ZB_SBZ, ZB_K, ZB_V, ZB_SGA, ZB_SGB = 0, 1, 2, 3, 4, 5, 6, 8
N_ZBLOCKS = 10
PJ_W = 2 * COL
PJ_LO_BLOCK = (1, 0, 4, 5, 8, 10)
PJ_HI_BLOCK = (2, 3, 7, 6, 9, 11)
PJ_U, PJ_AB_AZ, PJ_Q_BZ, PJ_K_V, PJ_GA, PJ_GB = 0, 1, 2, 3, 4, 5
N_PJ = 6


def _silu(x):
    return x * jax.nn.sigmoid(x)


def _mod_kernel(c_ref, w_ref, b_ref, o_ref):
    s = _silu(c_ref[...])
    o_ref[...] = jnp.dot(s.astype(BF16), w_ref[...].astype(BF16),
                         preferred_element_type=F32) + b_ref[...]


def _mod_call(c_all, w_ada, b_ada):
    nb, d = c_all.shape
    n = w_ada.shape[1]
    tn = 1024
    return pl.pallas_call(
        _mod_kernel,
        grid=(n // tn,),
        in_specs=[
            pl.BlockSpec((nb, d), lambda j: (0, 0)),
            pl.BlockSpec((d, tn), lambda j: (0, j)),
            pl.BlockSpec((1, tn), lambda j: (0, j)),
        ],
        out_specs=pl.BlockSpec((nb, tn), lambda j: (0, j)),
        out_shape=jax.ShapeDtypeStruct((nb, n), F32),
        compiler_params=pltpu.CompilerParams(
            dimension_semantics=("arbitrary",), vmem_limit_bytes=VMEM_LIMIT_BYTES),
        name="mod",
    )(c_all, w_ada, b_ada.reshape(1, n))


def _bias_kernel(rpb_ref, o_ref):
    h = pl.program_id(0)
    shape = (GRID_W, 2 * GRID_W)
    c = lax.broadcasted_iota(jnp.int32, shape, 0)
    l = lax.broadcasted_iota(jnp.int32, shape, 1)
    upper = l >= GRID_W
    kc = jnp.where(upper, l - GRID_W, l)
    cs = jnp.clip(c - WIN_C // 2, 0, GRID_W - WIN_C)
    valid = (kc >= cs) & (kc < cs + WIN_C)
    e_idx = kc - c + (WIN_C - 1)
    for d in range(2 * WIN_R - 2):
        acc = jnp.full(shape, MASK_VALUE, F32)
        for e in range(2 * WIN_C - 1):
            val = jnp.where(upper, rpb_ref[h, d + 1, e], rpb_ref[h, d, e]) * LOG2_E
            acc = jnp.where(valid & (e_idx == e), val, acc)
        o_ref[0, d] = acc


def _bias_call(rpb):
    nh = rpb.shape[0]
    nd = 2 * WIN_R - 2
    return pl.pallas_call(
        _bias_kernel,
        grid=(nh,),
        in_specs=[pl.BlockSpec(memory_space=pltpu.SMEM)],
        out_specs=pl.BlockSpec((1, nd, GRID_W, 2 * GRID_W), lambda h: (h, 0, 0, 0)),
        out_shape=jax.ShapeDtypeStruct((nh, nd, GRID_W, 2 * GRID_W), F32),
        compiler_params=pltpu.CompilerParams(dimension_semantics=("arbitrary",)),
        name="bias",
    )(rpb)


def _head_rms(o, gain):
    r = lax.broadcasted_iota(jnp.int32, (MXU_DIM, MXU_DIM), 0) // HEAD_DIM
    c = lax.broadcasted_iota(jnp.int32, (MXU_DIM, MXU_DIM), 1) // HEAD_DIM
    bd = jnp.where(r == c, 1.0, 0.0).astype(BF16)
    outs = []
    for k in range(o.shape[1] // MXU_DIM):
        oc = o[:, k * MXU_DIM:(k + 1) * MXU_DIM]
        ss = jnp.dot((oc * oc).astype(BF16), bd, preferred_element_type=F32)
        outs.append(oc * lax.rsqrt(ss * (1.0 / HEAD_DIM) + RMS_EPS))
    return jnp.concatenate(outs, axis=1) * gain


def _proj_kernel(x_ref, g_ref, mod_ref, wlo_ref, whi_ref, gain_ref, z_ref, h_scr,
                 *, tm, norm_rows, dot_rows, dot_cols):
    warm = tm // norm_rows
    s = pl.program_id(0)
    i = jnp.maximum(s - warm, 0) // N_PJ
    j = jnp.maximum(s - warm, 0) % N_PJ
    live = s >= warm
    cur = i % 2

    n_dot = tm // dot_rows
    sub_rows = norm_rows // n_dot

    def norm_rows_of_chunk(slot, chunk_row0, c):
        r0 = pl.multiple_of(c * sub_rows, sub_rows)
        x = x_ref[pl.ds(r0, sub_rows), :]
        ms = jnp.mean(x * x, axis=-1, keepdims=True)
        xn = x * lax.rsqrt(ms + RMS_EPS)
        h = (xn * g_ref[...]) * (1.0 + mod_ref[0, 1:2, :]) + mod_ref[0, 0:1, :]
        h_scr[slot, pl.ds(chunk_row0 + r0, sub_rows), :] = h.astype(BF16)

    @pl.when(s < warm)
    def _():
        chunk_row0 = pl.multiple_of(s * norm_rows, norm_rows)

        def body(c, carry):
            norm_rows_of_chunk(0, chunk_row0, c)
            return carry
        lax.fori_loop(0, n_dot, body, 0)

    def for_row_chunks(epilogue, norm_chunk=None):
        def body(c, carry):
            if norm_chunk is not None:
                norm_rows_of_chunk(1 - cur, norm_chunk * norm_rows, c)
            r0 = pl.multiple_of(c * dot_rows, dot_rows)
            rows = pl.ds(r0, dot_rows)
            hc = h_scr[cur, rows, :]
            for p0 in range(0, COL, dot_cols):
                cols = slice(p0, p0 + dot_cols)
                lo = jnp.dot(hc, wlo_ref[:, cols], preferred_element_type=F32)
                hi = jnp.dot(hc, whi_ref[:, cols], preferred_element_type=F32)
                z_lo, z_hi = epilogue(cols, lo, hi)
                if z_lo is not None:
                    z_ref[rows, cols] = z_lo.astype(BF16)
                if z_hi is not None:
                    z_ref[rows, slice(COL + p0, COL + p0 + dot_cols)] = z_hi.astype(BF16)
            return carry
        lax.fori_loop(0, n_dot, body, 0)

    def head_rms(cols, o):
        return _head_rms(o, gain_ref[0][:, cols])

    @pl.when(live & (j == PJ_U))
    def _():
        for_row_chunks(lambda cols, lo, hi: (lo * hi, None), PJ_U)

    @pl.when(live & (j == PJ_AB_AZ))
    def _():
        for_row_chunks(lambda cols, lo, hi: (None, lo * _silu(hi)), PJ_AB_AZ)

    @pl.when(live & (j == PJ_Q_BZ))
    def _():
        for_row_chunks(lambda cols, lo, hi: (head_rms(cols, lo), _silu(hi)), PJ_Q_BZ)

    @pl.when(live & (j == PJ_K_V))
    def _():
        for_row_chunks(lambda cols, lo, hi: (head_rms(cols, lo), hi), PJ_K_V)

    @pl.when(live & (j >= PJ_GA))
    def _():
        for_row_chunks(lambda cols, lo, hi: (jax.nn.sigmoid(lo), jax.nn.sigmoid(hi)))


def _proj_call(x2d, norm_g, mod3, w_in_bf, qk_gain, *, batch_of_tile, tm):
    t, d = x2d.shape
    assert w_in_bf.shape[1] == N_PJ * PJ_W and t % tm == 0
    n_tiles = t // tm
    norm_rows = tm // 4
    warm = tm // norm_rows
    assert warm == PJ_K_V + 1
    kern = functools.partial(_proj_kernel, tm=tm, norm_rows=norm_rows, dot_rows=512,
                             dot_cols=MXU_DIM)

    def tile_block(s):
        sl = jnp.maximum(s - warm, 0)
        return sl // N_PJ, sl % N_PJ

    def w_index(blocks):
        def index(s):
            j = tile_block(s)[1]
            col = blocks[-1]
            for step in range(N_PJ - 2, -1, -1):
                col = jnp.where(j == step, blocks[step], col)
            return 0, col
        return index

    def x_index(s):
        i, j = tile_block(s)
        nxt = jnp.minimum(i + 1, n_tiles - 1) * warm + jnp.minimum(j, warm - 1)
        return jnp.where(s < warm, s, nxt), 0

    def mod_index(s):
        i, _ = tile_block(s)
        nxt = batch_of_tile(jnp.minimum(i + 1, n_tiles - 1))
        return jnp.where(s < warm, batch_of_tile(0), nxt), 0, 0

    return pl.pallas_call(
        kern,
        grid=(warm + n_tiles * N_PJ,),
        in_specs=[
            pl.BlockSpec((norm_rows, d), x_index),
            pl.BlockSpec((1, d), lambda s: (0, 0)),
            pl.BlockSpec((1, 3, d), mod_index),
            pl.BlockSpec((d, COL), w_index(PJ_LO_BLOCK)),
            pl.BlockSpec((d, COL), w_index(PJ_HI_BLOCK)),
            pl.BlockSpec((1, 1, COL), lambda s: (jnp.where(tile_block(s)[1] == PJ_K_V, 1, 0), 0, 0)),
        ],
        out_specs=pl.BlockSpec((tm, PJ_W), lambda s: (tile_block(s)[0],
                                                      jnp.maximum(tile_block(s)[1] - 1, 0))),
        out_shape=jax.ShapeDtypeStruct((t, N_ZBLOCKS * COL), BF16),
        scratch_shapes=[pltpu.VMEM((2, tm, d), BF16)],
        compiler_params=pltpu.CompilerParams(
            dimension_semantics=("arbitrary",), vmem_limit_bytes=VMEM_LIMIT_BYTES),
        name="proj",
    )(x2d, norm_g.reshape(1, d), mod3, w_in_bf, w_in_bf, qk_gain)


def _attn_kernel(q_ref, bz_ref, t2_ref, z_hbm, o_ref, kv_buf, sem, *, rb, rows, seq):
    b = pl.program_id(0)
    i = pl.program_id(1)
    nblk = pl.num_programs(1)
    step = b * nblk + i
    slot = step % 2
    win_rows = rb + WIN_R
    nk = WIN_R * GRID_W

    def window_row0(blk):
        return jnp.clip(blk * rb - WIN_R // 2, 0, rows - win_rows)

    def kv_copy(bb, blk, sl):
        tok0 = pl.multiple_of(bb * seq + window_row0(blk) * GRID_W, GRID_W)
        return pltpu.make_async_copy(
            z_hbm.at[pl.ds(tok0, win_rows * GRID_W), pl.ds(ZB_K * COL, 2 * COL)],
            kv_buf.at[sl], sem.at[sl])

    @pl.when(step == 0)
    def _():
        kv_copy(b, i, slot).start()

    @pl.when(step + 1 < pl.num_programs(0) * nblk)
    def _():
        wrap = i == nblk - 1
        kv_copy(jnp.where(wrap, b + 1, b), jnp.where(wrap, 0, i + 1), 1 - slot).start()

    kv_copy(b, i, slot).wait()

    lane_head = lax.broadcasted_iota(jnp.int32, (GRID_W, MXU_DIM), 1) // HEAD_DIM
    row0 = window_row0(i)

    def row_body(jr, carry):
        r = i * rb + jr
        rs = jnp.clip(r - WIN_R // 2, 0, rows - WIN_R)
        start = pl.multiple_of((rs - row0) * GRID_W, GRID_W)
        dbase = rs - r + (WIN_R - 1)
        q_off = pl.multiple_of(jr * GRID_W, GRID_W)
        for g in range(N_GROUPS):
            cs = slice(g * MXU_DIM, (g + 1) * MXU_DIM)
            vs = slice(COL + g * MXU_DIM, COL + (g + 1) * MXU_DIM)
            qg = q_ref[pl.ds(q_off, GRID_W), cs]
            zero = jnp.zeros_like(qg)
            q4 = jnp.concatenate(
                [jnp.where(lane_head == h, qg, zero) for h in range(HEADS_PER_GROUP)], axis=0)
            kw = kv_buf[slot, pl.ds(start, nk), cs]
            s = lax.dot_general(q4, kw, (((1,), (1,)), ((), ())),
                                preferred_element_type=F32)
            cols = []
            for cb in range(nk // LANES):
                bias = jnp.concatenate(
                    [t2_ref[g * HEADS_PER_GROUP + h, dbase + 2 * cb]
                     for h in range(HEADS_PER_GROUP)], axis=0)
                cols.append(s[:, cb * LANES:(cb + 1) * LANES] + bias)
            s = jnp.concatenate(cols, axis=1)
            m = jnp.max(s, axis=-1, keepdims=True)
            p = jnp.exp2(s - m)
            den = jnp.sum(p, axis=-1, keepdims=True)
            vw = kv_buf[slot, pl.ds(start, nk), vs]
            pv = jnp.dot(p.astype(BF16), vw, preferred_element_type=F32)
            pv = pv * (1.0 / den)
            o = jnp.zeros((GRID_W, MXU_DIM), F32)
            for h in range(HEADS_PER_GROUP):
                o = o + jnp.where(lane_head == h, pv[h * GRID_W:(h + 1) * GRID_W], 0.0)
            sbz = bz_ref[pl.ds(q_off, GRID_W), cs].astype(F32)
            o_ref[pl.ds(q_off, GRID_W), cs] = (o * sbz).astype(BF16)
        return carry

    lax.fori_loop(0, rb, row_body, 0, unroll=4)


def _attn_call(z, t2, *, n_batch, seq, rb):
    rows = seq // GRID_W
    nblk = rows // rb
    tb = rb * GRID_W
    win_tokens = (rb + WIN_R) * GRID_W
    assert rows % rb == 0 and rows >= rb + WIN_R
    kern = functools.partial(_attn_kernel, rb=rb, rows=rows, seq=seq)

    def cur(col):
        return pl.BlockSpec((tb, COL), lambda b, i: (b * nblk + i, col))

    return pl.pallas_call(
        kern,
        grid=(n_batch, nblk),
        in_specs=[cur(ZB_Q), cur(ZB_SBZ),
                  pl.BlockSpec(t2.shape, lambda b, i: (0, 0, 0, 0)),
                  pl.BlockSpec(memory_space=pl.ANY)],
        out_specs=pl.BlockSpec((tb, COL), lambda b, i: (b * nblk + i, 0)),
        out_shape=jax.ShapeDtypeStruct((n_batch * seq, COL), BF16),
        scratch_shapes=[pltpu.VMEM((2, win_tokens, 2 * COL), BF16),
                        pltpu.SemaphoreType.DMA((2,))],
        compiler_params=pltpu.CompilerParams(
            dimension_semantics=("arbitrary", "arbitrary"),
            vmem_limit_bytes=VMEM_LIMIT_BYTES),
        name="attn",
    )(z, z, t2, z)


def _out_kernel(absz_ref, u_ref, up_ref, un_ref, yb_ref, sga_ref, sgb_ref, x_ref, mod_ref,
                cw_ref, cb_ref, wpa_ref, wpb_ref, wo_ref, o_ref, ya_scr,
                *, tm, n_tiles, tiles_per_seq, halo, sub_rows):
    s = pl.program_id(0)

    def conv_gate(slot):
        t = jnp.minimum(s, n_tiles - 1) % tiles_per_seq
        prev_row = jnp.where(t == 0, 0.0, up_ref[...].astype(F32)[halo - 1:halo, :])
        next_row = jnp.where(t == tiles_per_seq - 1, 0.0, un_ref[...].astype(F32)[0:1, :])
        sub = lax.broadcasted_iota(jnp.int32, (8, u_ref.shape[1]), 0)
        u = u_ref[...].astype(F32)
        r_m1 = pltpu.roll(u, 1, 0)
        r_p1 = pltpu.roll(u, tm - 1, 0)
        u_m1 = jnp.concatenate([jnp.where(sub == 0, prev_row, r_m1[:8]), r_m1[8:]], axis=0)
        u_p1 = jnp.concatenate([r_p1[:-8], jnp.where(sub == 7, next_row, r_p1[-8:])], axis=0)
        conv = u_m1 * cw_ref[0:1, :] + u * cw_ref[1:2, :] + u_p1 * cw_ref[2:3, :] + cb_ref[...]
        ya_scr[slot] = (absz_ref[...].astype(F32) * conv).astype(BF16)

    @pl.when(s == 0)
    def _():
        conv_gate(0)

    @pl.when(s > 0)
    def _():
        conv_gate(s % 2)
        prev = (s - 1) % 2
        for r0 in range(0, tm, sub_rows):
            rows = slice(r0, r0 + sub_rows)
            pb = jnp.dot(yb_ref[rows, :], wpb_ref[...], preferred_element_type=F32)
            ya = ya_scr[prev, rows, :]

            def merged(cols):
                pa = jnp.dot(ya, wpa_ref[:, cols], preferred_element_type=F32)
                return (sga_ref[rows, cols].astype(F32) * pa
                        + sgb_ref[rows, cols].astype(F32) * pb[:, cols]).astype(BF16)

            m0 = merged(slice(0, COL))
            m1 = merged(slice(COL, 2 * COL))
            o = (jnp.dot(m0, wo_ref[:COL, :], preferred_element_type=F32)
                 + jnp.dot(m1, wo_ref[COL:, :], preferred_element_type=F32))
            o_ref[rows, :] = x_ref[rows, :] + mod_ref[0, 2:3, :] * o


def _out_call(z, yb, x2d, mod3, conv_w, conv_b, wpa_bf, wpb_bf, wo_bf, *, batch_of_tile,
              seq, tm):
    t, d = x2d.shape
    dc = wpa_bf.shape[0]
    halo = 16
    tiles_per_seq = seq // tm
    n_tiles = t // tm
    hb = tm // halo
    nhalo = t // halo
    kern = functools.partial(_out_kernel, tm=tm, n_tiles=n_tiles, tiles_per_seq=tiles_per_seq,
                             halo=halo, sub_rows=256)

    def ahead(s):
        return jnp.minimum(s, n_tiles - 1)

    def behind(s):
        return jnp.maximum(s - 1, 0)

    const = lambda shape: pl.BlockSpec(shape, lambda s: (0,) * len(shape),
                                       pipeline_mode=pl.Buffered(1))
    return pl.pallas_call(
        kern,
        grid=(n_tiles + 1,),
        in_specs=[
            pl.BlockSpec((tm, COL), lambda s: (ahead(s), ZB_ABSZ)),
            pl.BlockSpec((tm, COL), lambda s: (ahead(s), ZB_U)),
            pl.BlockSpec((halo, COL), lambda s: (jnp.maximum(ahead(s) * hb - 1, 0), ZB_U)),
            pl.BlockSpec((halo, COL),
                         lambda s: (jnp.minimum((ahead(s) + 1) * hb, nhalo - 1), ZB_U)),
            pl.BlockSpec((tm, COL), lambda s: (behind(s), 0)),
            pl.BlockSpec((tm, 2 * COL), lambda s: (behind(s), ZB_SGA // 2)),
            pl.BlockSpec((tm, 2 * COL), lambda s: (behind(s), ZB_SGB // 2)),
            pl.BlockSpec((tm, d), lambda s: (behind(s), 0)),
            pl.BlockSpec((1, 3, d), lambda s: (batch_of_tile(behind(s)), 0, 0)),
            const((CONV_K, dc)), const((1, dc)),
            const((dc, d)), const((dc, d)), const((d, d)),
        ],
        out_specs=pl.BlockSpec((tm, d), lambda s: (behind(s), 0)),
        out_shape=jax.ShapeDtypeStruct((t, d), F32),
        scratch_shapes=[pltpu.VMEM((2, tm, dc), BF16)],
        compiler_params=pltpu.CompilerParams(
            dimension_semantics=("arbitrary",), vmem_limit_bytes=VMEM_LIMIT_BYTES),
        name="out",
    )(z, z, z, z, yb, z, z, x2d, mod3, conv_w, conv_b.reshape(1, dc),
      wpa_bf, wpb_bf, wo_bf)


def _stream(x, mod3, batch0, t2, norm_g, w_in_bf, qk_gain, conv_w, conv_b, wpa_bf, wpb_bf,
            wo_bf):
    nb, seq, d = x.shape
    x2d = x.reshape(nb * seq, d)
    tm_proj, tm_out, rb = 1024, 512, 8
    z = _proj_call(x2d, norm_g, mod3, w_in_bf, qk_gain, tm=tm_proj,
                   batch_of_tile=lambda i: batch0 + i // (seq // tm_proj))
    yb = _attn_call(z, t2, n_batch=nb, seq=seq, rb=rb)
    y = _out_call(z, yb, x2d, mod3, conv_w, conv_b, wpa_bf, wpb_bf, wo_bf, seq=seq, tm=tm_out,
                  batch_of_tile=lambda i: batch0 + i // (seq // tm_out))
    return y.reshape(nb, seq, d)


def kernel(x_prompt, x_sample, c_prompt, c_sample, norm_g, w_ada, b_ada, w_in, conv_w, conv_b,
           q_norm_g, k_norm_g, rpb, w_pa, w_pb, w_o):
    depth = w_in.shape[0]
    d = x_prompt.shape[-1]
    nbp = c_prompt.shape[0]
    c_all = jnp.concatenate([c_prompt, c_sample], axis=0)
    pad = (-c_all.shape[0]) % 8
    c_all = jnp.pad(c_all, ((0, pad), (0, 0)))
    y_prompt, y_sample = x_prompt, x_sample
    for l in range(depth):
        mod3 = _mod_call(c_all, w_ada[l], b_ada[l]).reshape(c_all.shape[0], 3, d)
        t2 = _bias_call(rpb[l])
        w_in_bf = w_in[l].astype(BF16)
        wpa_bf, wpb_bf, wo_bf = w_pa[l].astype(BF16), w_pb[l].astype(BF16), w_o[l].astype(BF16)
        qk_gain = jnp.stack([jnp.tile(q_norm_g[l] * (HEAD_DIM ** -0.5 * LOG2_E), N_HEADS),
                             jnp.tile(k_norm_g[l], N_HEADS)]).reshape(2, 1, COL)
        params = (t2, norm_g[l], w_in_bf, qk_gain, conv_w[l], conv_b[l], wpa_bf, wpb_bf, wo_bf)
        y_prompt = _stream(y_prompt, mod3, 0, *params)
        y_sample = _stream(y_sample, mod3, nbp, *params)
    return (y_prompt, y_sample)
```

```python
import functools

import jax
import jax.numpy as jnp
from jax import lax
from jax.experimental import pallas as pl
from jax.experimental.pallas import tpu as pltpu

F32 = jnp.float32
BF16 = jnp.bfloat16

GRID_W = 64
N_HEADS = 16
HEAD_DIM = 64
WIN_R = 8
WIN_C = 16
CONV_K = 3
RMS_EPS = 1e-6

LANES = 128
SUBLANES_F32 = 8
SUBLANES_BF16 = 16
MXU_DIM = 256
HEADS_PER_GROUP = MXU_DIM // HEAD_DIM
N_GROUPS = N_HEADS // HEADS_PER_GROUP
VMEM_LIMIT_BYTES = 56 * 1024 * 1024

MASK_VALUE = float("-inf")
LOG2_E = 1.4426950408889634

COL = 1024
ZB_U, ZB_ABSZ, ZB_Q, ZB_SBZ, ZB_K, ZB_V, ZB_SGA, ZB_SGB = 0, 1, 2, 3, 4, 5, 6, 8
N_ZBLOCKS = 10
PJ_W = 2 * COL
PJ_LO_BLOCK = (1, 0, 4, 5, 8, 10)
PJ_HI_BLOCK = (2, 3, 7, 6, 9, 11)
PJ_U, PJ_AB_AZ, PJ_Q_BZ, PJ_K_V, PJ_GA, PJ_GB = 0, 1, 2, 3, 4, 5
N_PJ = 6

PROJ_TILE_ROWS = 1024
PROJ_DOT_ROWS = 512
OUT_TILE_ROWS = 512
ATTN_BLOCK_GRID_ROWS = 8


def _silu(x):
    return x * jax.nn.sigmoid(x)


def _mod_kernel(c_ref, w_ref, b_ref, o_ref):
    s = _silu(c_ref[...])
    o_ref[...] = jnp.dot(s.astype(BF16), w_ref[...].astype(BF16),
                         preferred_element_type=F32) + b_ref[...]


def _mod_call(c_all, w_ada, b_ada):
    nb, d = c_all.shape
    n = w_ada.shape[1]
    tn = COL
    return pl.pallas_call(
        _mod_kernel,
        grid=(n // tn,),
        in_specs=[
            pl.BlockSpec((nb, d), lambda j: (0, 0)),
            pl.BlockSpec((d, tn), lambda j: (0, j)),
            pl.BlockSpec((1, tn), lambda j: (0, j)),
        ],
        out_specs=pl.BlockSpec((nb, tn), lambda j: (0, j)),
        out_shape=jax.ShapeDtypeStruct((nb, n), F32),
        compiler_params=pltpu.CompilerParams(
            dimension_semantics=("arbitrary",), vmem_limit_bytes=VMEM_LIMIT_BYTES),
        name="mod",
    )(c_all, w_ada, b_ada.reshape(1, n))


def _bias_kernel(rpb_ref, o_ref):
    h = pl.program_id(0)
    shape = (GRID_W, 2 * GRID_W)
    c = lax.broadcasted_iota(jnp.int32, shape, 0)
    l = lax.broadcasted_iota(jnp.int32, shape, 1)
    upper = l >= GRID_W
    kc = jnp.where(upper, l - GRID_W, l)
    cs = jnp.clip(c - WIN_C // 2, 0, GRID_W - WIN_C)
    valid = (kc >= cs) & (kc < cs + WIN_C)
    e_idx = jnp.where(valid, kc - c + (WIN_C - 1), -1)
    below = None
    for d in range(2 * WIN_R - 1):
        acc = jnp.full(shape, MASK_VALUE, F32)
        for e in range(2 * WIN_C - 1):
            acc = jnp.where(e_idx == e, rpb_ref[h, d, e] * LOG2_E, acc)
        if below is not None:
            o_ref[0, d - 1] = jnp.where(upper, acc, below)
        below = acc


def _bias_call(rpb):
    nh = rpb.shape[0]
    nd = 2 * WIN_R - 2
    return pl.pallas_call(
        _bias_kernel,
        grid=(nh,),
        in_specs=[pl.BlockSpec(memory_space=pltpu.SMEM)],
        out_specs=pl.BlockSpec((1, nd, GRID_W, 2 * GRID_W), lambda h: (h, 0, 0, 0)),
        out_shape=jax.ShapeDtypeStruct((nh, nd, GRID_W, 2 * GRID_W), F32),
        compiler_params=pltpu.CompilerParams(dimension_semantics=("arbitrary",)),
        name="bias",
    )(rpb)


def _head_rms(o, gain):
    r = lax.broadcasted_iota(jnp.int32, (MXU_DIM, MXU_DIM), 0) // HEAD_DIM
    c = lax.broadcasted_iota(jnp.int32, (MXU_DIM, MXU_DIM), 1) // HEAD_DIM
    bd = jnp.where(r == c, 1.0, 0.0).astype(BF16)
    outs = []
    for k in range(o.shape[1] // MXU_DIM):
        oc = o[:, k * MXU_DIM:(k + 1) * MXU_DIM]
        ss = jnp.dot((oc * oc).astype(BF16), bd, preferred_element_type=F32)
        outs.append(oc * lax.rsqrt(ss * (1.0 / HEAD_DIM) + RMS_EPS))
    return jnp.concatenate(outs, axis=1) * gain


def _proj_kernel(x_ref, g_ref, mod_ref, wlo_ref, whi_ref, gain_ref, z_ref, h_cur, h_next,
                 *, tm, norm_rows, dot_rows, dot_cols):
    warm = tm // norm_rows
    s = pl.program_id(0)
    i = jnp.maximum(s - warm, 0) // N_PJ
    j = jnp.maximum(s - warm, 0) % N_PJ
    live = s >= warm

    n_dot = tm // dot_rows
    sub_rows = norm_rows // n_dot

    def norm_rows_of_chunk(chunk_row0, c):
        r0 = pl.multiple_of(c * sub_rows, sub_rows)
        x = x_ref[pl.ds(r0, sub_rows), :]
        ms = jnp.mean(x * x, axis=-1, keepdims=True)
        xn = x * lax.rsqrt(ms + RMS_EPS)
        h = (xn * g_ref[...]) * (1.0 + mod_ref[0, 1:2, :]) + mod_ref[0, 0:1, :]
        h_next[pl.ds(chunk_row0 + r0, sub_rows), :] = h.astype(BF16)

    @pl.when(s < warm)
    def _():
        chunk_row0 = pl.multiple_of(s * norm_rows, norm_rows)

        def body(c, carry):
            norm_rows_of_chunk(chunk_row0, c)
            return carry
        lax.fori_loop(0, n_dot, body, 0)

    def for_row_chunks(epilogue, norm_chunk=None, split=1, unroll=1):
        piece_rows = dot_rows // split

        def body(c, carry):
            if norm_chunk is not None:
                norm_rows_of_chunk(norm_chunk * norm_rows, c)
            for piece in range(split):
                r0 = pl.multiple_of(c * dot_rows + piece * piece_rows, piece_rows)
                rows = pl.ds(r0, piece_rows)
                hc = h_cur[rows, :]
                for p0 in range(0, COL, dot_cols):
                    cols = slice(p0, p0 + dot_cols)
                    lo = jnp.dot(hc, wlo_ref[:, cols], preferred_element_type=F32)
                    hi = jnp.dot(hc, whi_ref[:, cols], preferred_element_type=F32)
                    z_lo, z_hi = epilogue(cols, lo, hi)
                    if z_lo is not None:
                        z_ref[rows, cols] = z_lo.astype(BF16)
                    if z_hi is not None:
                        z_ref[rows, slice(COL + p0, COL + p0 + dot_cols)] = z_hi.astype(BF16)
            return carry
        lax.fori_loop(0, n_dot, body, 0, unroll=unroll)

    def head_rms(cols, o):
        return _head_rms(o, gain_ref[0][:, cols])

    @pl.when(live & (j == PJ_U))
    def _():
        def body(c, carry):
            r0 = pl.multiple_of(c * norm_rows, norm_rows)
            h_cur[pl.ds(r0, norm_rows), :] = h_next[pl.ds(r0, norm_rows), :]
            return carry
        lax.fori_loop(0, warm, body, 0)

    @pl.when(live & (j == PJ_U))
    def _():
        for_row_chunks(lambda cols, lo, hi: (lo * hi, None), PJ_U)

    @pl.when(live & (j == PJ_AB_AZ))
    def _():
        for_row_chunks(lambda cols, lo, hi: (None, lo * _silu(hi)), PJ_AB_AZ)

    @pl.when(live & (j == PJ_Q_BZ))
    def _():
        for_row_chunks(lambda cols, lo, hi: (head_rms(cols, lo), _silu(hi)), PJ_Q_BZ)

    @pl.when(live & (j == PJ_K_V))
    def _():
        for_row_chunks(lambda cols, lo, hi: (head_rms(cols, lo), hi), PJ_K_V)

    @pl.when(live & (j >= PJ_GA))
    def _():
        for_row_chunks(lambda cols, lo, hi: (jax.nn.sigmoid(lo), jax.nn.sigmoid(hi)), split=2,
                       unroll=2)


def _proj_call(x2d, norm_g, mod3, w_in_bf, qk_gain, *, batch_of_tile, tm):
    t, d = x2d.shape
    assert w_in_bf.shape[1] == N_PJ * PJ_W and t % tm == 0
    n_tiles = t // tm
    norm_rows = tm // 4
    warm = tm // norm_rows
    assert warm == PJ_K_V + 1
    kern = functools.partial(_proj_kernel, tm=tm, norm_rows=norm_rows, dot_rows=PROJ_DOT_ROWS,
                             dot_cols=MXU_DIM)

    def tile_block(s):
        sl = jnp.maximum(s - warm, 0)
        return sl // N_PJ, sl % N_PJ

    def w_index(blocks):
        def index(s):
            j = tile_block(s)[1]
            col = blocks[-1]
            for step in range(N_PJ - 2, -1, -1):
                col = jnp.where(j == step, blocks[step], col)
            return 0, col
        return index

    def x_index(s):
        i, j = tile_block(s)
        nxt = jnp.minimum(i + 1, n_tiles - 1) * warm + jnp.minimum(j, warm - 1)
        return jnp.where(s < warm, s, nxt), 0

    def mod_index(s):
        i, _ = tile_block(s)
        nxt = batch_of_tile(jnp.minimum(i + 1, n_tiles - 1))
        return jnp.where(s < warm, batch_of_tile(0), nxt), 0, 0

    return pl.pallas_call(
        kern,
        grid=(warm + n_tiles * N_PJ,),
        in_specs=[
            pl.BlockSpec((norm_rows, d), x_index),
            pl.BlockSpec((1, d), lambda s: (0, 0)),
            pl.BlockSpec((1, 3, d), mod_index),
            pl.BlockSpec((d, COL), w_index(PJ_LO_BLOCK)),
            pl.BlockSpec((d, COL), w_index(PJ_HI_BLOCK)),
            pl.BlockSpec((1, 1, COL), lambda s: (jnp.where(tile_block(s)[1] == PJ_K_V, 1, 0), 0, 0)),
        ],
        out_specs=pl.BlockSpec((tm, PJ_W), lambda s: (tile_block(s)[0],
                                                      jnp.maximum(tile_block(s)[1] - 1, 0))),
        out_shape=jax.ShapeDtypeStruct((t, N_ZBLOCKS * COL), BF16),
        scratch_shapes=[pltpu.VMEM((tm, d), BF16), pltpu.VMEM((tm, d), BF16)],
        compiler_params=pltpu.CompilerParams(
            dimension_semantics=("arbitrary",), vmem_limit_bytes=VMEM_LIMIT_BYTES),
        name="proj",
    )(x2d, norm_g.reshape(1, d), mod3, w_in_bf, w_in_bf, qk_gain)


def _attn_kernel(q_ref, bz_ref, t2_ref, z_hbm, o_ref, kv_buf, sem, *, rb, rows, seq):
    b = pl.program_id(0)
    i = pl.program_id(1)
    nblk = pl.num_programs(1)
    step = b * nblk + i
    slot = step % 2
    win_rows = rb + WIN_R
    nk = WIN_R * GRID_W

    def window_row0(blk):
        return jnp.clip(blk * rb - WIN_R // 2, 0, rows - win_rows)

    def kv_copy(bb, blk, sl):
        tok0 = pl.multiple_of(bb * seq + window_row0(blk) * GRID_W, GRID_W)
        return pltpu.make_async_copy(
            z_hbm.at[pl.ds(tok0, win_rows * GRID_W), pl.ds(ZB_K * COL, 2 * COL)],
            kv_buf.at[sl], sem.at[sl])

    @pl.when(step == 0)
    def _():
        kv_copy(b, i, slot).start()

    @pl.when(step + 1 < pl.num_programs(0) * nblk)
    def _():
        wrap = i == nblk - 1
        kv_copy(jnp.where(wrap, b + 1, b), jnp.where(wrap, 0, i + 1), 1 - slot).start()

    kv_copy(b, i, slot).wait()

    lane_head = lax.broadcasted_iota(jnp.int32, (GRID_W, MXU_DIM), 1) // HEAD_DIM
    row0 = window_row0(i)

    def row_body(jr, carry):
        r = i * rb + jr
        rs = jnp.clip(r - WIN_R // 2, 0, rows - WIN_R)
        start = pl.multiple_of((rs - row0) * GRID_W, GRID_W)
        dbase = rs - r + (WIN_R - 1)
        q_off = pl.multiple_of(jr * GRID_W, GRID_W)
        for g in range(N_GROUPS):
            cs = slice(g * MXU_DIM, (g + 1) * MXU_DIM)
            vs = slice(COL + g * MXU_DIM, COL + (g + 1) * MXU_DIM)
            qg = q_ref[pl.ds(q_off, GRID_W), cs]
            zero = jnp.zeros_like(qg)
            q4 = jnp.concatenate(
                [jnp.where(lane_head == h, qg, zero) for h in range(HEADS_PER_GROUP)], axis=0)
            kw = kv_buf[slot, pl.ds(start, nk), cs]
            s = lax.dot_general(q4, kw, (((1,), (1,)), ((), ())),
                                preferred_element_type=F32)
            cols = []
            for cb in range(nk // LANES):
                bias = jnp.concatenate(
                    [t2_ref[g * HEADS_PER_GROUP + h, dbase + 2 * cb]
                     for h in range(HEADS_PER_GROUP)], axis=0)
                cols.append(s[:, cb * LANES:(cb + 1) * LANES] + bias)
            s = jnp.concatenate(cols, axis=1)
            m = jnp.max(s, axis=-1, keepdims=True)
            p = jnp.exp2(s - m)
            den = jnp.sum(p, axis=-1, keepdims=True)
            vw = kv_buf[slot, pl.ds(start, nk), vs]
            pv = jnp.dot(p.astype(BF16), vw, preferred_element_type=F32)
            pv = pv * (1.0 / den)
            o = jnp.zeros((GRID_W, MXU_DIM), F32)
            for h in range(HEADS_PER_GROUP):
                o = o + jnp.where(lane_head == h, pv[h * GRID_W:(h + 1) * GRID_W], 0.0)
            sbz = bz_ref[pl.ds(q_off, GRID_W), cs].astype(F32)
            o_ref[pl.ds(q_off, GRID_W), cs] = (o * sbz).astype(BF16)
        return carry

    lax.fori_loop(0, rb, row_body, 0, unroll=8)


def _attn_call(z, t2, *, n_batch, seq, rb):
    rows = seq // GRID_W
    nblk = rows // rb
    tb = rb * GRID_W
    win_tokens = (rb + WIN_R) * GRID_W
    assert rows % rb == 0 and rows >= rb + WIN_R
    kern = functools.partial(_attn_kernel, rb=rb, rows=rows, seq=seq)

    def cur(col):
        return pl.BlockSpec((tb, COL), lambda b, i: (b * nblk + i, col))

    return pl.pallas_call(
        kern,
        grid=(n_batch, nblk),
        in_specs=[cur(ZB_Q), cur(ZB_SBZ),
                  pl.BlockSpec(t2.shape, lambda b, i: (0, 0, 0, 0)),
                  pl.BlockSpec(memory_space=pl.ANY)],
        out_specs=pl.BlockSpec((tb, COL), lambda b, i: (b * nblk + i, 0)),
        out_shape=jax.ShapeDtypeStruct((n_batch * seq, COL), BF16),
        scratch_shapes=[pltpu.VMEM((2, win_tokens, 2 * COL), BF16),
                        pltpu.SemaphoreType.DMA((2,))],
        compiler_params=pltpu.CompilerParams(
            dimension_semantics=("arbitrary", "arbitrary"),
            vmem_limit_bytes=VMEM_LIMIT_BYTES),
        name="attn",
    )(z, z, t2, z)


def _out_kernel(absz_ref, u_ref, up_ref, un_ref, yb_ref, sga_ref, sgb_ref, x_ref, mod_ref,
                cw_ref, cb_ref, wpa_ref, wpb_ref, wo_ref, o_ref, ya_scr,
                *, tm, n_tiles, tiles_per_seq, halo):
    s = pl.program_id(0)

    def conv_gate(slot):
        t = jnp.minimum(s, n_tiles - 1) % tiles_per_seq
        prev_row = jnp.where(t == 0, 0.0, up_ref[...].astype(F32)[halo - 1:halo, :])
        next_row = jnp.where(t == tiles_per_seq - 1, 0.0, un_ref[...].astype(F32)[0:1, :])
        sub = lax.broadcasted_iota(jnp.int32, (SUBLANES_F32, u_ref.shape[1]), 0)
        u = u_ref[...].astype(F32)
        r_m1 = pltpu.roll(u, 1, 0)
        r_p1 = pltpu.roll(u, tm - 1, 0)
        g = SUBLANES_F32
        u_m1 = jnp.concatenate([jnp.where(sub == 0, prev_row, r_m1[:g]), r_m1[g:]], axis=0)
        u_p1 = jnp.concatenate([r_p1[:-g], jnp.where(sub == g - 1, next_row, r_p1[-g:])], axis=0)
        conv = u_m1 * cw_ref[0:1, :] + u * cw_ref[1:2, :] + u_p1 * cw_ref[2:3, :] + cb_ref[...]
        ya_scr[slot] = (absz_ref[...].astype(F32) * conv).astype(BF16)

    @pl.when(s == 0)
    def _():
        conv_gate(0)

    @pl.when(s > 0)
    def _():
        conv_gate(s % 2)
        pb = jnp.dot(yb_ref[...], wpb_ref[...], preferred_element_type=F32)
        ya = ya_scr[(s - 1) % 2]

        def merged(cols):
            pa = jnp.dot(ya, wpa_ref[:, cols], preferred_element_type=F32)
            return (sga_ref[:, cols].astype(F32) * pa
                    + sgb_ref[:, cols].astype(F32) * pb[:, cols]).astype(BF16)

        m0 = merged(slice(0, COL))
        m1 = merged(slice(COL, 2 * COL))
        o = (jnp.dot(m0, wo_ref[:COL, :], preferred_element_type=F32)
             + jnp.dot(m1, wo_ref[COL:, :], preferred_element_type=F32))
        o_ref[...] = x_ref[...] + mod_ref[0, 2:3, :] * o


def _out_call(z, yb, x2d, mod3, conv_w, conv_b, wpa_bf, wpb_bf, wo_bf, *, batch_of_tile,
              seq, tm):
    t, d = x2d.shape
    dc = wpa_bf.shape[0]
    halo = SUBLANES_BF16
    tiles_per_seq = seq // tm
    n_tiles = t // tm
    hb = tm // halo
    nhalo = t // halo
    kern = functools.partial(_out_kernel, tm=tm, n_tiles=n_tiles, tiles_per_seq=tiles_per_seq,
                             halo=halo)

    def ahead(s):
        return jnp.minimum(s, n_tiles - 1)

    def behind(s):
        return jnp.maximum(s - 1, 0)

    const = lambda shape: pl.BlockSpec(shape, lambda s: (0,) * len(shape),
                                       pipeline_mode=pl.Buffered(1))
    return pl.pallas_call(
        kern,
        grid=(n_tiles + 1,),
        in_specs=[
            pl.BlockSpec((tm, COL), lambda s: (ahead(s), ZB_ABSZ)),
            pl.BlockSpec((tm, COL), lambda s: (ahead(s), ZB_U)),
            pl.BlockSpec((halo, COL), lambda s: (jnp.maximum(ahead(s) * hb - 1, 0), ZB_U)),
            pl.BlockSpec((halo, COL),
                         lambda s: (jnp.minimum((ahead(s) + 1) * hb, nhalo - 1), ZB_U)),
            pl.BlockSpec((tm, COL), lambda s: (behind(s), 0)),
            pl.BlockSpec((tm, 2 * COL), lambda s: (behind(s), ZB_SGA // 2)),
            pl.BlockSpec((tm, 2 * COL), lambda s: (behind(s), ZB_SGB // 2)),
            pl.BlockSpec((tm, d), lambda s: (behind(s), 0)),
            pl.BlockSpec((1, 3, d), lambda s: (batch_of_tile(behind(s)), 0, 0)),
            const((CONV_K, dc)), const((1, dc)),
            const((dc, d)), const((dc, d)), const((d, d)),
        ],
        out_specs=pl.BlockSpec((tm, d), lambda s: (behind(s), 0)),
        out_shape=jax.ShapeDtypeStruct((t, d), F32),
        scratch_shapes=[pltpu.VMEM((2, tm, dc), BF16)],
        compiler_params=pltpu.CompilerParams(
            dimension_semantics=("arbitrary",), vmem_limit_bytes=VMEM_LIMIT_BYTES),
        name="out",
    )(z, z, z, z, yb, z, z, x2d, mod3, conv_w, conv_b.reshape(1, dc),
      wpa_bf, wpb_bf, wo_bf)


def _stream(x, mod3, batch0, t2, norm_g, w_in_bf, qk_gain, conv_w, conv_b, wpa_bf, wpb_bf,
            wo_bf):
    nb, seq, d = x.shape
    x2d = x.reshape(nb * seq, d)
    tm_proj, tm_out, rb = PROJ_TILE_ROWS, OUT_TILE_ROWS, ATTN_BLOCK_GRID_ROWS
    z = _proj_call(x2d, norm_g, mod3, w_in_bf, qk_gain, tm=tm_proj,
                   batch_of_tile=lambda i: batch0 + i // (seq // tm_proj))
    yb = _attn_call(z, t2, n_batch=nb, seq=seq, rb=rb)
    y = _out_call(z, yb, x2d, mod3, conv_w, conv_b, wpa_bf, wpb_bf, wo_bf, seq=seq, tm=tm_out,
                  batch_of_tile=lambda i: batch0 + i // (seq // tm_out))
    return y.reshape(nb, seq, d)


def kernel(x_prompt, x_sample, c_prompt, c_sample, norm_g, w_ada, b_ada, w_in, conv_w, conv_b,
           q_norm_g, k_norm_g, rpb, w_pa, w_pb, w_o):
    depth = w_in.shape[0]
    d = x_prompt.shape[-1]
    nbp = c_prompt.shape[0]
    c_all = jnp.concatenate([c_prompt, c_sample], axis=0)
    pad = (-c_all.shape[0]) % 8
    c_all = jnp.pad(c_all, ((0, pad), (0, 0)))
    y_prompt, y_sample = x_prompt, x_sample
    for l in range(depth):
        mod3 = _mod_call(c_all, w_ada[l], b_ada[l]).reshape(c_all.shape[0], 3, d)
        t2 = _bias_call(rpb[l])
        w_in_bf = w_in[l].astype(BF16)
        wpa_bf, wpb_bf, wo_bf = w_pa[l].astype(BF16), w_pb[l].astype(BF16), w_o[l].astype(BF16)
        qk_gain = jnp.stack([jnp.tile(q_norm_g[l] * (HEAD_DIM ** -0.5 * LOG2_E), N_HEADS),
                             jnp.tile(k_norm_g[l], N_HEADS)]).reshape(2, 1, COL)
        params = (t2, norm_g[l], w_in_bf, qk_gain, conv_w[l], conv_b[l], wpa_bf, wpb_bf, wo_bf)
        y_prompt = _stream(y_prompt, mod3, 0, *params)
        y_sample = _stream(y_sample, mod3, nbp, *params)
    return (y_prompt, y_sample)
```
